```python
import math
import jax, jax.numpy as jnp
from jax import lax
import numpy as np

D_MODEL = 1024
BATCH = 32
SEQ = 2048
DEPTH = 4

N_META = 16
ATTN_HEADS = 4
ATTN_DK = 64
ATTN_DV = 2 * ATTN_DK
ATTN_WIDTH = ATTN_HEADS * ATTN_DV
CONV_WIDTH = D_MODEL - ATTN_WIDTH
CONV_K = 3
MIX_WIDTH = ATTN_WIDTH + CONV_WIDTH
Q_COLS = ATTN_HEADS * 2 * ATTN_DK
K_COLS = Q_COLS
V_COLS = ATTN_WIDTH
IN_COLS = Q_COLS + K_COLS + V_COLS + 3 * CONV_WIDTH
Q_BLOCK = 128
REL_BUCKETS = 32
REL_MAX_DIST = 128
PEER_HEADS = 8
PEER_NKEYS = 128
PEER_N = PEER_NKEYS * PEER_NKEYS
PEER_DQ = 256
PEER_TOPK = 16
PEER_CHUNK = 256
DEEPNORM_ALPHA = (2 * DEPTH) ** 0.25
DEEPNORM_BETA = (8 * DEPTH) ** -0.25
LN_EPS = 1e-5
RMS_EPS = 1e-5
NEG_BIG = -1e30

kernel_name = "hymba_diffattn_shortconv_peer_deepnorm"


def layer_norm(x, g, b):
    xf = x.astype(jnp.float32)
    mu = xf.mean(-1, keepdims=True)
    var = jnp.square(xf - mu).mean(-1, keepdims=True)
    return ((xf - mu) * lax.rsqrt(var + LN_EPS) * g.astype(jnp.float32) + b.astype(jnp.float32)).astype(x.dtype)


def t5_bucket(qpos, kpos):
    n = jnp.maximum(qpos[:, None] - kpos[None, :], 0)
    max_exact = REL_BUCKETS // 2
    nf = jnp.maximum(n, 1).astype(jnp.float32)
    large = max_exact + (jnp.log(nf / max_exact) / math.log(REL_MAX_DIST / max_exact)
                         * (REL_BUCKETS - max_exact)).astype(jnp.int32)
    large = jnp.minimum(large, REL_BUCKETS - 1)
    return jnp.where(n < max_exact, n, large)


def diff_attn_block(qb, qpos, k, v, kpos, rel_bias, lam):
    logits = jnp.einsum('bqhmd,bkhmd->bhmqk', qb.astype(jnp.float32), k.astype(jnp.float32)) * (ATTN_DK ** -0.5)
    bias = rel_bias.astype(jnp.float32)[t5_bucket(qpos, kpos)]
    bias = jnp.transpose(bias, (2, 0, 1))[None, :, None]
    mask = kpos[None, :] <= qpos[:, None]
    logits = jnp.where(mask, logits + bias, NEG_BIG)
    p = jax.nn.softmax(logits, axis=-1)
    attn = p[:, :, 0] - lam * p[:, :, 1]
    return jnp.einsum('bhqk,bkhd->bqhd', attn, v.astype(jnp.float32))


def differential_attention(q, k, v, rel_bias, lam, subln_g, lam_init):
    Bn, L = q.shape[0], q.shape[1]
    kpos = jnp.arange(L, dtype=jnp.int32)
    mpos = jnp.arange(N_META, dtype=jnp.int32)
    out_meta = diff_attn_block(q[:, :N_META], mpos, k[:, :N_META], v[:, :N_META], mpos, rel_bias, lam)
    n_blk = (L - N_META) // Q_BLOCK
    q_real = q[:, N_META:].reshape(Bn, n_blk, Q_BLOCK, ATTN_HEADS, 2, ATTN_DK)
    q_real = jnp.moveaxis(q_real, 1, 0)

    def one_block(args):
        qb, bi = args
        qpos = N_META + bi * Q_BLOCK + jnp.arange(Q_BLOCK, dtype=jnp.int32)
        return diff_attn_block(qb, qpos, k, v, kpos, rel_bias, lam)

    out_real = lax.map(one_block, (q_real, jnp.arange(n_blk, dtype=jnp.int32)))
    out_real = jnp.moveaxis(out_real, 0, 1).reshape(Bn, L - N_META, ATTN_HEADS, ATTN_DV)
    out = jnp.concatenate([out_meta, out_real], axis=1)
    out = out * lax.rsqrt(jnp.square(out).mean(-1, keepdims=True) + RMS_EPS)
    out = out * subln_g.astype(jnp.float32) * (1.0 - lam_init)
    return out.reshape(Bn, L, ATTN_WIDTH).astype(q.dtype)


def short_conv(z, w):
    return lax.conv_general_dilated(z, w[:, None, :].astype(z.dtype), window_strides=(1,),
                                    padding=[(CONV_K - 1, 0)],
                                    dimension_numbers=('NWC', 'WIO', 'NWC'),
                                    feature_group_count=CONV_WIDTH)


def peer(x, w_q, sub_keys, u, v):
    Bn, L, _ = x.shape
    T = Bn * L
    n_chunks = -(-T // PEER_CHUNK)
    pad = n_chunks * PEER_CHUNK - T
    xt = jnp.pad(x.reshape(T, D_MODEL), ((0, pad), (0, 0))).reshape(n_chunks, PEER_CHUNK, D_MODEL)

    def one_chunk(xc):
        q = (xc @ w_q).reshape(PEER_CHUNK, PEER_HEADS, 2, PEER_DQ // 2)
        s = jnp.einsum('thpd,hpnd->thpn', q.astype(jnp.float32), sub_keys.astype(jnp.float32))
        s1, i1 = lax.top_k(s[:, :, 0], PEER_TOPK)
        s2, i2 = lax.top_k(s[:, :, 1], PEER_TOPK)
        cand = (s1[..., :, None] + s2[..., None, :]).reshape(PEER_CHUNK, PEER_HEADS, PEER_TOPK * PEER_TOPK)
        sc, ci = lax.top_k(cand, PEER_TOPK)
        idx = (jnp.take_along_axis(i1, ci // PEER_TOPK, axis=-1) * PEER_NKEYS
               + jnp.take_along_axis(i2, ci % PEER_TOPK, axis=-1))
        g = jax.nn.softmax(sc, axis=-1)
        ue = u[idx]
        ve = v[idx]
        act = jax.nn.gelu(jnp.einsum('td,thkd->thk', xc, ue), approximate=False)
        return jnp.einsum('thk,thkd->td', (g * act).astype(xc.dtype), ve)

    out = lax.map(one_chunk, xt)
    return out.reshape(n_chunks * PEER_CHUNK, D_MODEL)[:T].reshape(Bn, L, D_MODEL)


def setup_inputs(seed: int = 0) -> dict:
    key = jax.random.key(seed)
    ks = jax.random.split(key, 24)
    f32 = jnp.float32
    nrm = lambda k, shape, s: jax.random.normal(k, shape, f32) * s
    col_scale = jnp.concatenate([
        jnp.ones((Q_COLS + K_COLS,), f32), jnp.full((V_COLS,), DEEPNORM_BETA, f32),
        jnp.ones((2 * CONV_WIDTH,), f32), jnp.full((CONV_WIDTH,), DEEPNORM_BETA, f32)])
    return {
        "x": nrm(ks[0], (BATCH, SEQ, D_MODEL), 1.0),
        "meta_tokens": nrm(ks[1], (N_META, D_MODEL), 1.0),
        "ln_in_g": 1.0 + nrm(ks[2], (D_MODEL,), 0.02),
        "ln_in_b": nrm(ks[3], (D_MODEL,), 0.02),
        "rel_bias": nrm(ks[4], (REL_BUCKETS, ATTN_HEADS), 0.5),
        "w_in": nrm(ks[5], (DEPTH, D_MODEL, IN_COLS), D_MODEL ** -0.5) * col_scale,
        "conv_w": nrm(ks[6], (DEPTH, CONV_K, CONV_WIDTH), CONV_K ** -0.5),
        "lambda_q1": nrm(ks[7], (DEPTH, ATTN_DK), 0.1),
        "lambda_k1": nrm(ks[8], (DEPTH, ATTN_DK), 0.1),
        "lambda_q2": nrm(ks[9], (DEPTH, ATTN_DK), 0.1),
        "lambda_k2": nrm(ks[10], (DEPTH, ATTN_DK), 0.1),
        "subln_g": 1.0 + nrm(ks[11], (DEPTH, ATTN_DV), 0.02),
        "w_out": nrm(ks[12], (DEPTH, MIX_WIDTH, D_MODEL), MIX_WIDTH ** -0.5) * DEEPNORM_BETA,
        "ln1_g": 1.0 + nrm(ks[13], (DEPTH, D_MODEL), 0.02),
        "ln1_b": nrm(ks[14], (DEPTH, D_MODEL), 0.02),
        "peer_w_q": nrm(ks[15], (DEPTH, D_MODEL, PEER_HEADS * PEER_DQ), D_MODEL ** -0.5),
        "peer_sub_keys": nrm(ks[16], (DEPTH, PEER_HEADS, 2, PEER_NKEYS, PEER_DQ // 2), (PEER_DQ // 2) ** -0.5),
        "peer_u": nrm(ks[17], (DEPTH, PEER_N, D_MODEL), D_MODEL ** -0.5),
        "peer_v": nrm(ks[18], (DEPTH, PEER_N, D_MODEL), DEEPNORM_BETA * PEER_HEADS ** -0.5),
        "ln2_g": 1.0 + nrm(ks[19], (DEPTH, D_MODEL), 0.02),
        "ln2_b": nrm(ks[20], (DEPTH, D_MODEL), 0.02),
    }


def reference(x, meta_tokens, ln_in_g, ln_in_b, rel_bias, w_in, conv_w, lambda_q1, lambda_k1,
              lambda_q2, lambda_k2, subln_g, w_out, ln1_g, ln1_b, peer_w_q, peer_sub_keys,
              peer_u, peer_v, ln2_g, ln2_b):
    Bn = x.shape[0]
    meta = jnp.broadcast_to(meta_tokens[None].astype(x.dtype), (Bn, N_META, D_MODEL))
    h = jnp.concatenate([meta, x], axis=1)
    h = layer_norm(h, ln_in_g, ln_in_b)
    L = h.shape[1]
    splits = [Q_COLS, Q_COLS + K_COLS, Q_COLS + K_COLS + V_COLS,
              Q_COLS + K_COLS + V_COLS + CONV_WIDTH, Q_COLS + K_COLS + V_COLS + 2 * CONV_WIDTH]
    for l in range(DEPTH):
        lam_init = 0.8 - 0.6 * math.exp(-0.3 * l)
        lam = (jnp.exp(jnp.dot(lambda_q1[l].astype(jnp.float32), lambda_k1[l].astype(jnp.float32)))
               - jnp.exp(jnp.dot(lambda_q2[l].astype(jnp.float32), lambda_k2[l].astype(jnp.float32)))
               + lam_init)
        proj = h @ w_in[l]
        q, k, v, gb, gc, z = jnp.split(proj, splits, axis=-1)
        q = q.reshape(Bn, L, ATTN_HEADS, 2, ATTN_DK)
        k = k.reshape(Bn, L, ATTN_HEADS, 2, ATTN_DK)
        v = v.reshape(Bn, L, ATTN_HEADS, ATTN_DV)
        attn_out = differential_attention(q, k, v, rel_bias, lam, subln_g[l], lam_init)
        conv_out = gb * short_conv(gc * z, conv_w[l])
        mix = jnp.concatenate([attn_out, conv_out], axis=-1) @ w_out[l]
        h = layer_norm(DEEPNORM_ALPHA * h + mix, ln1_g[l], ln1_b[l])
        ffn = peer(h, peer_w_q[l], peer_sub_keys[l], peer_u[l], peer_v[l])
        h = layer_norm(DEEPNORM_ALPHA * h + ffn, ln2_g[l], ln2_b[l])
    return h[:, N_META:]
```

```python
import functools
import math

import jax
import jax.numpy as jnp
from jax import lax
from jax.experimental import pallas as pl
from jax.experimental.pallas import tpu as pltpu

F32 = jnp.float32
BF16 = jnp.bfloat16

D_MODEL = 1024
DEPTH = 4
N_META = 16
META_ROWS = 128
ATTN_HEADS = 4
ATTN_DK = 64
ATTN_DV = 2 * ATTN_DK
ATTN_WIDTH = ATTN_HEADS * ATTN_DV
CONV_WIDTH = D_MODEL - ATTN_WIDTH
CONV_K = 3
IN_COLS = 3 * ATTN_WIDTH + 3 * CONV_WIDTH
REL_BUCKETS = 32
REL_MAX_DIST = 128
PEER_HEADS = 8
PEER_NKEYS = 128
PEER_N = PEER_NKEYS * PEER_NKEYS
PEER_DQ = 256
PEER_TOPK = 16
DEEPNORM_ALPHA = (2 * DEPTH) ** 0.25
LN_EPS = 1e-5
RMS_EPS = 1e-5
NEG_BIG = -1e30
SQRT_HALF = math.sqrt(0.5)

LANES = 128
VMEM_LIMIT = 56 * 1024 * 1024

_NT = (((1,), (1,)), ((), ()))


def _ln(y, g, b):
    mu = jnp.mean(y, axis=-1, keepdims=True)
    yc = y - mu
    var = jnp.mean(yc * yc, axis=-1, keepdims=True)
    return yc * lax.rsqrt(var + LN_EPS) * g + b


def _ln_kernel(x_ref, g_ref, b_ref, o_ref):
    o_ref[...] = _ln(x_ref[...], g_ref[...], b_ref[...])


def _layer_norm_rows(x, g, b, tm):
    T = x.shape[0]
    return pl.pallas_call(
        _ln_kernel,
        grid=(T // tm,),
        in_specs=[pl.BlockSpec((tm, D_MODEL), lambda i: (i, 0)),
                  pl.BlockSpec((1, D_MODEL), lambda i: (0, 0)),
                  pl.BlockSpec((1, D_MODEL), lambda i: (0, 0))],
        out_specs=pl.BlockSpec((tm, D_MODEL), lambda i: (i, 0)),
        out_shape=jax.ShapeDtypeStruct((T, D_MODEL), F32),
        compiler_params=pltpu.CompilerParams(dimension_semantics=("parallel",)),
        name="ln_in",
    )(x, g.reshape(1, -1), b.reshape(1, -1))


def _proj_kernel(h_ref, w_ref, cw_ref, halo_ref, q_ref, k_ref, v_ref, c_ref, *rest, ts, emit_u):
    if emit_u:
        u_ref, ubuf = rest
    else:
        (ubuf,) = rest
    W = ATTN_WIDTH

    @pl.when(pl.program_id(1) == 0)
    def _():
        ubuf[0:8, :] = halo_ref[...]

    hb = h_ref[...].astype(BF16)

    def col(c):
        return jnp.dot(hb, w_ref[:, c * W:(c + 1) * W], preferred_element_type=F32)

    q_ref[...] = (col(0) * (ATTN_DK ** -0.5)).astype(BF16)
    k_ref[...] = col(1).astype(BF16)
    v_ref[...] = col(2).astype(BF16)
    u = col(4) * col(5)
    ubuf[8:8 + ts, :] = u
    if emit_u:
        u_ref[...] = u
    conv = (cw_ref[0:1, :] * ubuf[6:6 + ts, :] + cw_ref[1:2, :] * ubuf[7:7 + ts, :]
            + cw_ref[2:3, :] * u)
    c_ref[...] = (col(3) * conv).astype(BF16)
    ubuf[0:8, :] = ubuf[ts:ts + 8, :]


def _proj_conv(h, w_in, conv_w, halo, Bn, S, ts, emit_u):
    T = Bn * S
    nS = S // ts
    row = lambda b, s: (b * nS + s, 0)
    full = lambda b, s: (0, 0)
    out_shape = [jax.ShapeDtypeStruct((T, ATTN_WIDTH), BF16)] * 4
    out_specs = [pl.BlockSpec((ts, ATTN_WIDTH), row)] * 4
    if emit_u:
        out_shape = out_shape + [jax.ShapeDtypeStruct((T, CONV_WIDTH), F32)]
        out_specs = out_specs + [pl.BlockSpec((ts, CONV_WIDTH), row)]
    return pl.pallas_call(
        functools.partial(_proj_kernel, ts=ts, emit_u=emit_u),
        grid=(Bn, nS),
        in_specs=[pl.BlockSpec((ts, D_MODEL), row),
                  pl.BlockSpec((D_MODEL, IN_COLS), full),
                  pl.BlockSpec((CONV_K, CONV_WIDTH), full),
                  pl.BlockSpec((8, CONV_WIDTH), full)],
        out_specs=out_specs,
        out_shape=out_shape,
        scratch_shapes=[pltpu.VMEM((ts + 8, CONV_WIDTH), F32)],
        compiler_params=pltpu.CompilerParams(
            dimension_semantics=("parallel", "arbitrary"), vmem_limit_bytes=VMEM_LIMIT),
        name="proj_conv",
    )(h, w_in, conv_w, halo)


def _bias_kernel(rb_ref, d_ref, p_ref, *, t):
    h = pl.program_id(0)
    far = rb_ref[REL_BUCKETS - 1, h]

    def table(n):
        max_exact = REL_BUCKETS // 2
        nf = jnp.maximum(n, 1).astype(F32)
        large = max_exact + (jnp.log(nf / max_exact) / math.log(REL_MAX_DIST / max_exact)
                             * (REL_BUCKETS - max_exact)).astype(jnp.int32)
        large = jnp.minimum(large, REL_BUCKETS - 1)
        bucket = jnp.where(n < max_exact, n, large)
        out = jnp.zeros(n.shape, F32)
        for bkt in range(REL_BUCKETS):
            out = jnp.where(bucket == bkt, rb_ref[bkt, h], out)
        return out - far

    r = lax.broadcasted_iota(jnp.int32, (t, t), 0)
    c = lax.broadcasted_iota(jnp.int32, (t, t), 1)
    d_ref[0, 0] = jnp.where(c <= r, table(jnp.maximum(r - c, 0)), NEG_BIG)
    d_ref[0, 1] = table(t + r - c)
    rp = lax.broadcasted_iota(jnp.int32, (t, LANES), 0)
    cp = lax.broadcasted_iota(jnp.int32, (t, LANES), 1)
    p_ref[0, 0] = jnp.where(cp < N_META, table(N_META + rp - cp), NEG_BIG)
    p_ref[0, 1] = jnp.where(cp < N_META, 0.0, NEG_BIG)


def _bias_tables(rel_bias, t):
    return pl.pallas_call(
        functools.partial(_bias_kernel, t=t),
        grid=(ATTN_HEADS,),
        in_specs=[pl.BlockSpec(memory_space=pltpu.SMEM)],
        out_specs=[pl.BlockSpec((1, 2, t, t), lambda h: (h, 0, 0, 0)),
                   pl.BlockSpec((1, 2, t, LANES), lambda h: (h, 0, 0, 0))],
        out_shape=[jax.ShapeDtypeStruct((ATTN_HEADS, 2, t, t), F32),
                   jax.ShapeDtypeStruct((ATTN_HEADS, 2, t, LANES), F32)],
        name="bias_tables",
    )(rel_bias)


def _attn_kernel(q_ref, k_ref, v_ref, kp_ref, vp_ref, d_ref, p_ref, lam_ref, g_ref, o_ref,
                 *, t, has_prefix, lam_init):
    i = pl.program_id(2)
    q = q_ref[...]
    qs = (q[:, :ATTN_DK], q[:, ATTN_DK:])

    def step(state, kt, vt, bias):
        new = []
        for m in range(2):
            mx, l, acc = state[m]
            s = lax.dot_general(qs[m], kt[:, m * ATTN_DK:(m + 1) * ATTN_DK], _NT,
                                preferred_element_type=F32)
            if bias is not None:
                s = s + bias
            mn = jnp.maximum(mx, jnp.max(s, axis=-1, keepdims=True))
            a = jnp.exp(mx - mn)
            p = jnp.exp(s - mn)
            l = a * l + jnp.sum(p, axis=-1, keepdims=True)
            acc = a * acc + jnp.dot(p.astype(BF16), vt, preferred_element_type=F32)
            new.append((mn, l, acc))
        return tuple(new)

    def tile(j):
        r0 = pl.multiple_of(j * t, t)
        return k_ref[pl.ds(r0, t), :], v_ref[pl.ds(r0, t), :]

    init = (jnp.full((t, 1), -jnp.inf, F32), jnp.zeros((t, 1), F32), jnp.zeros((t, ATTN_DV), F32))
    state = (init, init)
    if has_prefix:
        state = step(state, kp_ref[...], vp_ref[...], p_ref[0, jnp.minimum(i, 1)])

    def far_body(j, st):
        kt, vt = tile(j)
        return step(st, kt, vt, None)

    state = lax.fori_loop(0, jnp.maximum(i - 1, 0), far_body, state)

    def near_body(j, st):
        kt, vt = tile(j)
        return step(st, kt, vt, d_ref[0, i - j])

    state = lax.fori_loop(jnp.maximum(i - 1, 0), i + 1, near_body, state)

    lam = lam_ref[0, 0]
    (_, l0, a0), (_, l1, a1) = state
    out = a0 / l0 - lam * (a1 / l1)
    out = out * lax.rsqrt(jnp.mean(out * out, axis=-1, keepdims=True) + RMS_EPS)
    o_ref[...] = (out * g_ref[...] * (1.0 - lam_init)).astype(BF16)


def _diff_attention(q, k, v, kp, vp, dtab, ptab, lam, subln_g, Bn, S, t, has_prefix, lam_init):
    nq = S // t
    return pl.pallas_call(
        functools.partial(_attn_kernel, t=t, has_prefix=has_prefix, lam_init=lam_init),
        grid=(Bn, ATTN_HEADS, nq),
        in_specs=[pl.BlockSpec((t, ATTN_DV), lambda b, h, i: (b * nq + i, h)),
                  pl.BlockSpec((S, ATTN_DV), lambda b, h, i: (b, h)),
                  pl.BlockSpec((S, ATTN_DV), lambda b, h, i: (b, h)),
                  pl.BlockSpec((META_ROWS, ATTN_DV), lambda b, h, i: (0, h)),
                  pl.BlockSpec((META_ROWS, ATTN_DV), lambda b, h, i: (0, h)),
                  pl.BlockSpec((1, 2, t, t), lambda b, h, i: (h, 0, 0, 0)),
                  pl.BlockSpec((1, 2, t, LANES), lambda b, h, i: (h, 0, 0, 0)),
                  pl.BlockSpec(memory_space=pltpu.SMEM),
                  pl.BlockSpec((1, ATTN_DV), lambda b, h, i: (0, 0))],
        out_specs=pl.BlockSpec((t, ATTN_DV), lambda b, h, i: (b * nq + i, h)),
        out_shape=jax.ShapeDtypeStruct((Bn * S, ATTN_WIDTH), BF16),
        compiler_params=pltpu.CompilerParams(
            dimension_semantics=("parallel", "parallel", "arbitrary"), vmem_limit_bytes=VMEM_LIMIT),
        name="diff_attn",
    )(q, k, v, kp, vp, dtab, ptab, lam, subln_g.reshape(1, -1))


def _lam_kernel(q1_ref, k1_ref, q2_ref, k2_ref, o_ref, *, lam_init):
    a = jnp.sum(q1_ref[...] * k1_ref[...], axis=-1, keepdims=True)
    b = jnp.sum(q2_ref[...] * k2_ref[...], axis=-1, keepdims=True)
    o_ref[...] = jnp.broadcast_to(jnp.exp(a) - jnp.exp(b) + lam_init, o_ref.shape)


def _lambda(q1, k1, q2, k2, lam_init):
    r = lambda x: x.reshape(1, ATTN_DK)
    return pl.pallas_call(
        functools.partial(_lam_kernel, lam_init=lam_init),
        out_shape=jax.ShapeDtypeStruct((1, LANES), F32),
        name="lambda",
    )(r(q1), r(k1), r(q2), r(k2))


def _outproj_kernel(a_ref, c_ref, h_ref, w_ref, g_ref, b_ref, o_ref):
    mix = jnp.dot(a_ref[...], w_ref[0:ATTN_WIDTH, :], preferred_element_type=F32)
    mix = mix + jnp.dot(c_ref[...], w_ref[ATTN_WIDTH:, :], preferred_element_type=F32)
    o_ref[...] = _ln(DEEPNORM_ALPHA * h_ref[...] + mix, g_ref[...], b_ref[...])


def _outproj_ln(a, c, h, w_out, g, b, tm):
    T = h.shape[0]
    row = lambda i: (i, 0)
    full = lambda i: (0, 0)
    return pl.pallas_call(
        _outproj_kernel,
        grid=(T // tm,),
        in_specs=[pl.BlockSpec((tm, ATTN_WIDTH), row),
                  pl.BlockSpec((tm, CONV_WIDTH), row),
                  pl.BlockSpec((tm, D_MODEL), row),
                  pl.BlockSpec((D_MODEL, D_MODEL), full),
                  pl.BlockSpec((1, D_MODEL), full),
                  pl.BlockSpec((1, D_MODEL), full)],
        out_specs=pl.BlockSpec((tm, D_MODEL), row),
        out_shape=jax.ShapeDtypeStruct((T, D_MODEL), F32),
        compiler_params=pltpu.CompilerParams(
            dimension_semantics=("parallel",), vmem_limit_bytes=VMEM_LIMIT),
        name="outproj_ln",
    )(a, c, h, w_out, g.reshape(1, -1), b.reshape(1, -1))


def _top16(s):
    vals = []
    cur = s
    for _ in range(PEER_TOPK):
        m = jnp.max(cur, axis=0, keepdims=True)
        vals.append(m)
        cur = jnp.where(cur == m, -jnp.inf, cur)
    return vals


def _pair_threshold(c1, c2):
    c1s = jnp.concatenate(c1, axis=0)
    c2s = jnp.concatenate(c2, axis=0)
    lo = c2s[0:8, :]
    row = lax.broadcasted_iota(jnp.int32, lo.shape, 0)
    parts = [c1s + c2[0], c1s[0:8, :] + c2[1], c1s[0:8, :] + c2[2], c1s[0:8, :] + c2[3],
             c1[0] + c2s[8:16, :]]
    for i in range(3):
        parts.append(jnp.where(row >= 4, c1[i] + lo, -jnp.inf))
    cand = jnp.concatenate(parts, axis=0)
    cur = cand
    tau = None
    for _ in range(PEER_TOPK):
        tau = jnp.max(cur, axis=0, keepdims=True)
        cur = jnp.where(cur == tau, -jnp.inf, cur)
    top = c1[0] + c2[0]
    z = jnp.sum(jnp.where(cand >= tau, jnp.exp(cand - top), 0.0), axis=0, keepdims=True)
    return tau, z


def _peer_kernel(h_ref, wq_ref, sk_ref, u_ref, vt_ref, g_ref, b_ref, o_ref,
                 hb_ref, qp_ref, s_ref, e_ref, tau_ref, act_ref, wact_ref, acc_ref, *, tT, E):
    j = pl.program_id(1)
    nl = tT // LANES
    na = E // PEER_NKEYS
    RB = 16

    @pl.when(j == 0)
    def _():
        hb = h_ref[...].astype(BF16)
        hb_ref[...] = hb
        QC = 512
        for c in range(PEER_HEADS * PEER_DQ // QC):
            qp_ref[c * QC:(c + 1) * QC, :] = lax.dot_general(
                wq_ref[c * QC:(c + 1) * QC, :], hb, _NT, preferred_element_type=F32).astype(BF16)
        half = PEER_DQ // 2
        for hp in range(2 * PEER_HEADS):
            sT = jnp.dot(sk_ref[hp], qp_ref[hp * half:(hp + 1) * half, :], preferred_element_type=F32)
            for l in range(nl):
                s_ref[hp, l] = sT[:, l * LANES:(l + 1) * LANES]

        def select(idx, carry):
            h = idx // nl
            l = idx % nl
            s1 = s_ref[2 * h, l]
            s2 = s_ref[2 * h + 1, l]
            c1 = _top16(s1)
            c2 = _top16(s2)
            tau, z = _pair_threshold(c1, c2)
            e_ref[2 * h, l] = jnp.exp(s1 - c1[0]) / z
            e_ref[2 * h + 1, l] = jnp.exp(s2 - c2[0])
            tau_ref[h, l] = jnp.broadcast_to(tau, (8, LANES))
            return carry

        lax.fori_loop(0, PEER_HEADS * nl, select, 0)
        acc_ref[...] = jnp.zeros_like(acc_ref)

    act_ref[...] = lax.dot_general(u_ref[...], hb_ref[...], _NT, preferred_element_type=F32)

    def dense(r, carry):
        b0 = pl.multiple_of(r * RB, RB)
        for l in range(nl):
            lanes = slice(l * LANES, (l + 1) * LANES)
            s2c = [s_ref[2 * h + 1, l, pl.ds(b0, RB), :] for h in range(PEER_HEADS)]
            e2c = [e_ref[2 * h + 1, l, pl.ds(b0, RB), :] for h in range(PEER_HEADS)]
            taus = [tau_ref[h, l, 0:1, :] for h in range(PEER_HEADS)]
            for aa in range(na):
                a = j * na + aa
                w = jnp.zeros((RB, LANES), F32)
                for h in range(PEER_HEADS):
                    s1r = s_ref[2 * h, l, pl.ds(a, 1), :]
                    e1r = e_ref[2 * h, l, pl.ds(a, 1), :]
                    w = w + jnp.where(s1r + s2c[h] >= taus[h], e1r * e2c[h], 0.0)
                x = act_ref[pl.ds(aa * PEER_NKEYS + b0, RB), lanes]
                gelu = 0.5 * x * (1.0 + lax.erf(x * SQRT_HALF))
                wact_ref[pl.ds(aa * PEER_NKEYS + b0, RB), lanes] = (w * gelu).astype(BF16)
        return carry

    lax.fori_loop(0, PEER_NKEYS // RB, dense, 0)
    acc_ref[...] += jnp.dot(vt_ref[...], wact_ref[...], preferred_element_type=F32)

    @pl.when(j == pl.num_programs(1) - 1)
    def _():
        y = DEEPNORM_ALPHA * h_ref[...] + acc_ref[...].T
        o_ref[...] = _ln(y, g_ref[...], b_ref[...])


def _peer_ln(h, wq_t, sk, u, v_t, g, b, tT, E):
    T = h.shape[0]
    nl = tT // LANES
    row = lambda i, j: (i, 0)
    full2 = lambda i, j: (0, 0)
    return pl.pallas_call(
        functools.partial(_peer_kernel, tT=tT, E=E),
        grid=(T // tT, PEER_N // E),
        in_specs=[pl.BlockSpec((tT, D_MODEL), row),
                  pl.BlockSpec((PEER_HEADS * PEER_DQ, D_MODEL), full2),
                  pl.BlockSpec((2 * PEER_HEADS, PEER_NKEYS, PEER_DQ // 2), lambda i, j: (0, 0, 0)),
                  pl.BlockSpec((E, D_MODEL), lambda i, j: (j, 0)),
                  pl.BlockSpec((D_MODEL, E), lambda i, j: (0, j)),
                  pl.BlockSpec((1, D_MODEL), full2),
                  pl.BlockSpec((1, D_MODEL), full2)],
        out_specs=pl.BlockSpec((tT, D_MODEL), row),
        out_shape=jax.ShapeDtypeStruct((T, D_MODEL), F32),
        scratch_shapes=[pltpu.VMEM((tT, D_MODEL), BF16),
                        pltpu.VMEM((PEER_HEADS * PEER_DQ, tT), BF16),
                        pltpu.VMEM((2 * PEER_HEADS, nl, PEER_NKEYS, LANES), F32),
                        pltpu.VMEM((2 * PEER_HEADS, nl, PEER_NKEYS, LANES), F32),
                        pltpu.VMEM((PEER_HEADS, nl, 8, LANES), F32),
                        pltpu.VMEM((E, tT), F32),
                        pltpu.VMEM((E, tT), BF16),
                        pltpu.VMEM((D_MODEL, tT), F32)],
        compiler_params=pltpu.CompilerParams(
            dimension_semantics=("parallel", "arbitrary"), vmem_limit_bytes=VMEM_LIMIT),
        name="peer_ln",
    )(h, wq_t, sk, u, v_t, g.reshape(1, -1), b.reshape(1, -1))


def _tile(n, pref):
    return pref if n % pref == 0 else n


def _run_layer(l, h, Bn, S, prefix, params, tabs, lam, last_meta=False):
    (w_in, conv_w, subln_g, w_out, ln1_g, ln1_b, wq_t, sk, u, v_t, ln2_g, ln2_b) = params
    lam_init = 0.8 - 0.6 * math.exp(-0.3 * l)
    is_meta = prefix is None
    t = _tile(S, 256)
    ts = _tile(S, 512)
    halo = jnp.zeros((8, CONV_WIDTH), F32) if is_meta else prefix[2]
    outs = _proj_conv(h, w_in, conv_w, halo, Bn, S, ts, emit_u=is_meta)
    q, k, v, c = outs[:4]
    new_prefix = (k, v, outs[4][N_META - 8:N_META]) if is_meta else None
    if last_meta:
        return None, new_prefix
    kp, vp = (k, v) if is_meta else prefix[:2]
    a = _diff_attention(q, k, v, kp, vp, tabs[0], tabs[1], lam, subln_g, Bn, S, t,
                        has_prefix=not is_meta, lam_init=lam_init)
    T = Bn * S
    h1 = _outproj_ln(a, c, h, w_out, ln1_g, ln1_b, _tile(T, 512))
    h2 = _peer_ln(h1, wq_t, sk, u, v_t, ln2_g, ln2_b, _tile(T, 512), 512)
    return h2, new_prefix


def kernel(x, meta_tokens, ln_in_g, ln_in_b, rel_bias, w_in, conv_w, lambda_q1, lambda_k1, lambda_q2,
           lambda_k2, subln_g, w_out, ln1_g, ln1_b, peer_w_q, peer_sub_keys, peer_u, peer_v, ln2_g, ln2_b):
    Bn, S, _ = x.shape
    depth = w_in.shape[0]
    w_in_b = w_in.astype(BF16)
    w_out_b = w_out.astype(BF16)
    wq_t = jnp.swapaxes(peer_w_q, 1, 2).astype(BF16)
    sk = peer_sub_keys.reshape(depth, 2 * PEER_HEADS, PEER_NKEYS, PEER_DQ // 2).astype(BF16)
    u_b = peer_u.astype(BF16)
    v_t = jnp.swapaxes(peer_v, 1, 2).astype(BF16)

    t_main = _tile(S, 256)
    tabs_main = _bias_tables(rel_bias, t_main)
    tabs_meta = _bias_tables(rel_bias, META_ROWS)

    meta = jnp.zeros((META_ROWS, D_MODEL), F32).at[:N_META].set(meta_tokens.astype(F32))
    hm = _layer_norm_rows(meta, ln_in_g, ln_in_b, META_ROWS)
    h = _layer_norm_rows(x.reshape(Bn * S, D_MODEL), ln_in_g, ln_in_b, _tile(Bn * S, 512))

    for l in range(depth):
        lam_init = 0.8 - 0.6 * math.exp(-0.3 * l)
        lam = _lambda(lambda_q1[l], lambda_k1[l], lambda_q2[l], lambda_k2[l], lam_init)
        params = (w_in_b[l], conv_w[l], subln_g[l], w_out_b[l], ln1_g[l], ln1_b[l], wq_t[l], sk[l],
                  u_b[l], v_t[l], ln2_g[l], ln2_b[l])
        hm, prefix = _run_layer(l, hm, 1, META_ROWS, None, params, tabs_meta, lam,
                                last_meta=(l == depth - 1))
        h, _ = _run_layer(l, h, Bn, S, prefix, params, tabs_main, lam)
    return h.reshape(Bn, S, D_MODEL)
```

```python
import functools
import math

import jax
import jax.numpy as jnp
from jax import lax
from jax.experimental import pallas as pl
from jax.experimental.pallas import tpu as pltpu

F32 = jnp.float32
BF16 = jnp.bfloat16

D_MODEL = 1024
DEPTH = 4
N_META = 16
META_ROWS = 128
ATTN_HEADS = 4
ATTN_DK = 64
ATTN_DV = 2 * ATTN_DK
ATTN_WIDTH = ATTN_HEADS * ATTN_DV
CONV_WIDTH = D_MODEL - ATTN_WIDTH
CONV_K = 3
IN_COLS = 3 * ATTN_WIDTH + 3 * CONV_WIDTH
REL_BUCKETS = 32
REL_MAX_DIST = 128
PEER_HEADS = 8
PEER_NKEYS = 128
PEER_N = PEER_NKEYS * PEER_NKEYS
PEER_DQ = 256
PEER_TOPK = 16
DEEPNORM_ALPHA = (2 * DEPTH) ** 0.25
LN_EPS = 1e-5
RMS_EPS = 1e-5
NEG_BIG = -1e30
SQRT_HALF = math.sqrt(0.5)

LANES = 128
VMEM_LIMIT = 56 * 1024 * 1024

_NT = (((1,), (1,)), ((), ()))


def _ln(y, g, b):
    mu = jnp.mean(y, axis=-1, keepdims=True)
    yc = y - mu
    var = jnp.mean(yc * yc, axis=-1, keepdims=True)
    return yc * lax.rsqrt(var + LN_EPS) * g + b


def _ln_kernel(x_ref, g_ref, b_ref, o_ref):
    o_ref[...] = _ln(x_ref[...], g_ref[...], b_ref[...])


def _layer_norm_rows(x, g, b, tm):
    T = x.shape[0]
    return pl.pallas_call(
        _ln_kernel,
        grid=(T // tm,),
        in_specs=[pl.BlockSpec((tm, D_MODEL), lambda i: (i, 0)),
                  pl.BlockSpec((1, D_MODEL), lambda i: (0, 0)),
                  pl.BlockSpec((1, D_MODEL), lambda i: (0, 0))],
        out_specs=pl.BlockSpec((tm, D_MODEL), lambda i: (i, 0)),
        out_shape=jax.ShapeDtypeStruct((T, D_MODEL), F32),
        compiler_params=pltpu.CompilerParams(dimension_semantics=("parallel",)),
        name="ln_in",
    )(x, g.reshape(1, -1), b.reshape(1, -1))


def _proj_kernel(h_ref, w_ref, cw_ref, halo_ref, q_ref, k_ref, v_ref, c_ref, *rest, ts, emit_u):
    if emit_u:
        u_ref, ubuf = rest
    else:
        (ubuf,) = rest
    W = ATTN_WIDTH

    @pl.when(pl.program_id(1) == 0)
    def _():
        ubuf[0:8, :] = halo_ref[...]

    hb = h_ref[...].astype(BF16)

    def col(c):
        return jnp.dot(hb, w_ref[:, c * W:(c + 1) * W], preferred_element_type=F32)

    q_ref[...] = (col(0) * (ATTN_DK ** -0.5)).astype(BF16)
    k_ref[...] = col(1).astype(BF16)
    v_ref[...] = col(2).astype(BF16)
    u = col(4) * col(5)
    ubuf[8:8 + ts, :] = u
    if emit_u:
        u_ref[...] = u
    conv = (cw_ref[0:1, :] * ubuf[6:6 + ts, :] + cw_ref[1:2, :] * ubuf[7:7 + ts, :]
            + cw_ref[2:3, :] * u)
    c_ref[...] = (col(3) * conv).astype(BF16)
    ubuf[0:8, :] = ubuf[ts:ts + 8, :]


def _proj_conv(h, w_in, conv_w, halo, Bn, S, ts, emit_u):
    T = Bn * S
    nS = S // ts
    row = lambda b, s: (b * nS + s, 0)
    full = lambda b, s: (0, 0)
    out_shape = [jax.ShapeDtypeStruct((T, ATTN_WIDTH), BF16)] * 4
    out_specs = [pl.BlockSpec((ts, ATTN_WIDTH), row)] * 4
    if emit_u:
        out_shape = out_shape + [jax.ShapeDtypeStruct((T, CONV_WIDTH), F32)]
        out_specs = out_specs + [pl.BlockSpec((ts, CONV_WIDTH), row)]
    return pl.pallas_call(
        functools.partial(_proj_kernel, ts=ts, emit_u=emit_u),
        grid=(Bn, nS),
        in_specs=[pl.BlockSpec((ts, D_MODEL), row),
                  pl.BlockSpec((D_MODEL, IN_COLS), full),
                  pl.BlockSpec((CONV_K, CONV_WIDTH), full),
                  pl.BlockSpec((8, CONV_WIDTH), full)],
        out_specs=out_specs,
        out_shape=out_shape,
        scratch_shapes=[pltpu.VMEM((ts + 8, CONV_WIDTH), F32)],
        compiler_params=pltpu.CompilerParams(
            dimension_semantics=("parallel", "arbitrary"), vmem_limit_bytes=VMEM_LIMIT),
        name="proj_conv",
    )(h, w_in, conv_w, halo)


def _bias_kernel(rb_ref, d_ref, p_ref, *, t):
    h = pl.program_id(0)
    far = rb_ref[REL_BUCKETS - 1, h]

    def table(n):
        max_exact = REL_BUCKETS // 2
        nf = jnp.maximum(n, 1).astype(F32)
        large = max_exact + (jnp.log(nf / max_exact) / math.log(REL_MAX_DIST / max_exact)
                             * (REL_BUCKETS - max_exact)).astype(jnp.int32)
        large = jnp.minimum(large, REL_BUCKETS - 1)
        bucket = jnp.where(n < max_exact, n, large)
        out = jnp.zeros(n.shape, F32)
        for bkt in range(REL_BUCKETS):
            out = jnp.where(bucket == bkt, rb_ref[bkt, h], out)
        return out - far

    r = lax.broadcasted_iota(jnp.int32, (t, t), 0)
    c = lax.broadcasted_iota(jnp.int32, (t, t), 1)
    d_ref[0, 0] = jnp.where(c <= r, table(jnp.maximum(r - c, 0)), NEG_BIG)
    d_ref[0, 1] = table(t + r - c)
    rp = lax.broadcasted_iota(jnp.int32, (t, LANES), 0)
    cp = lax.broadcasted_iota(jnp.int32, (t, LANES), 1)
    p_ref[0, 0] = jnp.where(cp < N_META, table(N_META + rp - cp), NEG_BIG)
    p_ref[0, 1] = jnp.where(cp < N_META, 0.0, NEG_BIG)


def _bias_tables(rel_bias, t):
    return pl.pallas_call(
        functools.partial(_bias_kernel, t=t),
        grid=(ATTN_HEADS,),
        in_specs=[pl.BlockSpec(memory_space=pltpu.SMEM)],
        out_specs=[pl.BlockSpec((1, 2, t, t), lambda h: (h, 0, 0, 0)),
                   pl.BlockSpec((1, 2, t, LANES), lambda h: (h, 0, 0, 0))],
        out_shape=[jax.ShapeDtypeStruct((ATTN_HEADS, 2, t, t), F32),
                   jax.ShapeDtypeStruct((ATTN_HEADS, 2, t, LANES), F32)],
        name="bias_tables",
    )(rel_bias)


def _attn_kernel(q_ref, k_ref, v_ref, kp_ref, vp_ref, d_ref, p_ref, lam_ref, g_ref, o_ref,
                 *, t, has_prefix, lam_init):
    i = pl.program_id(2)
    q = q_ref[...]
    qs = (q[:, :ATTN_DK], q[:, ATTN_DK:])

    def step(state, kt, vt, bias):
        new = []
        for m in range(2):
            mx, l, acc = state[m]
            s = lax.dot_general(qs[m], kt[:, m * ATTN_DK:(m + 1) * ATTN_DK], _NT,
                                preferred_element_type=F32)
            if bias is not None:
                s = s + bias
            mn = jnp.maximum(mx, jnp.max(s, axis=-1, keepdims=True))
            a = jnp.exp(mx - mn)
            p = jnp.exp(s - mn)
            l = a * l + jnp.sum(p, axis=-1, keepdims=True)
            acc = a * acc + jnp.dot(p.astype(BF16), vt, preferred_element_type=F32)
            new.append((mn, l, acc))
        return tuple(new)

    def tile(j):
        r0 = pl.multiple_of(j * t, t)
        return k_ref[pl.ds(r0, t), :], v_ref[pl.ds(r0, t), :]

    init = (jnp.full((t, 1), -jnp.inf, F32), jnp.zeros((t, 1), F32), jnp.zeros((t, ATTN_DV), F32))
    state = (init, init)
    if has_prefix:
        state = step(state, kp_ref[...], vp_ref[...], p_ref[0, jnp.minimum(i, 1)])

    def far_body(j, st):
        kt, vt = tile(j)
        return step(st, kt, vt, None)

    state = lax.fori_loop(0, jnp.maximum(i - 1, 0), far_body, state)

    def near_body(j, st):
        kt, vt = tile(j)
        return step(st, kt, vt, d_ref[0, i - j])

    state = lax.fori_loop(jnp.maximum(i - 1, 0), i + 1, near_body, state)

    lam = lam_ref[0, 0]
    (_, l0, a0), (_, l1, a1) = state
    out = a0 / l0 - lam * (a1 / l1)
    out = out * lax.rsqrt(jnp.mean(out * out, axis=-1, keepdims=True) + RMS_EPS)
    o_ref[...] = (out * g_ref[...] * (1.0 - lam_init)).astype(BF16)


def _diff_attention(q, k, v, kp, vp, dtab, ptab, lam, subln_g, Bn, S, t, has_prefix, lam_init):
    nq = S // t
    return pl.pallas_call(
        functools.partial(_attn_kernel, t=t, has_prefix=has_prefix, lam_init=lam_init),
        grid=(Bn, ATTN_HEADS, nq),
        in_specs=[pl.BlockSpec((t, ATTN_DV), lambda b, h, i: (b * nq + i, h)),
                  pl.BlockSpec((S, ATTN_DV), lambda b, h, i: (b, h)),
                  pl.BlockSpec((S, ATTN_DV), lambda b, h, i: (b, h)),
                  pl.BlockSpec((META_ROWS, ATTN_DV), lambda b, h, i: (0, h)),
                  pl.BlockSpec((META_ROWS, ATTN_DV), lambda b, h, i: (0, h)),
                  pl.BlockSpec((1, 2, t, t), lambda b, h, i: (h, 0, 0, 0)),
                  pl.BlockSpec((1, 2, t, LANES), lambda b, h, i: (h, 0, 0, 0)),
                  pl.BlockSpec(memory_space=pltpu.SMEM),
                  pl.BlockSpec((1, ATTN_DV), lambda b, h, i: (0, 0))],
        out_specs=pl.BlockSpec((t, ATTN_DV), lambda b, h, i: (b * nq + i, h)),
        out_shape=jax.ShapeDtypeStruct((Bn * S, ATTN_WIDTH), BF16),
        compiler_params=pltpu.CompilerParams(
            dimension_semantics=("parallel", "parallel", "arbitrary"), vmem_limit_bytes=VMEM_LIMIT),
        name="diff_attn",
    )(q, k, v, kp, vp, dtab, ptab, lam, subln_g.reshape(1, -1))


def _lam_kernel(q1_ref, k1_ref, q2_ref, k2_ref, o_ref, *, lam_init):
    a = jnp.sum(q1_ref[...] * k1_ref[...], axis=-1, keepdims=True)
    b = jnp.sum(q2_ref[...] * k2_ref[...], axis=-1, keepdims=True)
    o_ref[...] = jnp.broadcast_to(jnp.exp(a) - jnp.exp(b) + lam_init, o_ref.shape)


def _lambda(q1, k1, q2, k2, lam_init):
    r = lambda x: x.reshape(1, ATTN_DK)
    return pl.pallas_call(
        functools.partial(_lam_kernel, lam_init=lam_init),
        out_shape=jax.ShapeDtypeStruct((1, LANES), F32),
        name="lambda",
    )(r(q1), r(k1), r(q2), r(k2))


def _outproj_kernel(a_ref, c_ref, h_ref, w_ref, g_ref, b_ref, o_ref):
    mix = jnp.dot(a_ref[...], w_ref[0:ATTN_WIDTH, :], preferred_element_type=F32)
    mix = mix + jnp.dot(c_ref[...], w_ref[ATTN_WIDTH:, :], preferred_element_type=F32)
    o_ref[...] = _ln(DEEPNORM_ALPHA * h_ref[...] + mix, g_ref[...], b_ref[...])


def _outproj_ln(a, c, h, w_out, g, b, tm):
    T = h.shape[0]
    row = lambda i: (i, 0)
    full = lambda i: (0, 0)
    return pl.pallas_call(
        _outproj_kernel,
        grid=(T // tm,),
        in_specs=[pl.BlockSpec((tm, ATTN_WIDTH), row),
                  pl.BlockSpec((tm, CONV_WIDTH), row),
                  pl.BlockSpec((tm, D_MODEL), row),
                  pl.BlockSpec((D_MODEL, D_MODEL), full),
                  pl.BlockSpec((1, D_MODEL), full),
                  pl.BlockSpec((1, D_MODEL), full)],
        out_specs=pl.BlockSpec((tm, D_MODEL), row),
        out_shape=jax.ShapeDtypeStruct((T, D_MODEL), F32),
        compiler_params=pltpu.CompilerParams(
            dimension_semantics=("parallel",), vmem_limit_bytes=VMEM_LIMIT),
        name="outproj_ln",
    )(a, c, h, w_out, g.reshape(1, -1), b.reshape(1, -1))


NOT_SELECTED = 64.0


def _top16(s, with_rank):
    vals = []
    cur = s
    rank = jnp.full(s.shape, NOT_SELECTED, F32) if with_rank else None
    for i in range(PEER_TOPK):
        m = jnp.max(cur, axis=0, keepdims=True)
        vals.append(m)
        hit = cur == m
        if with_rank:
            rank = jnp.where(hit, float(i), rank)
        cur = jnp.where(hit, -jnp.inf, cur)
    return vals, rank


def _dup_bf16(x):
    b = pltpu.bitcast(x, jnp.uint32)
    hi = (b + jnp.uint32(0x7FFF) + ((b >> 16) & jnp.uint32(1))) & jnp.uint32(0xFFFF0000)
    return hi | (hi >> 16)


def _pair_threshold(c1, c2):
    c1s = jnp.concatenate(c1, axis=0)
    c2s = jnp.concatenate(c2, axis=0)
    lo = c2s[0:8, :]
    row = lax.broadcasted_iota(jnp.int32, lo.shape, 0)
    parts = [c1s + c2[0], c1s[0:8, :] + c2[1], c1s[0:8, :] + c2[2], c1s[0:8, :] + c2[3],
             c1[0] + c2s[8:16, :]]
    for i in range(3):
        parts.append(jnp.where(row >= 4, c1[i] + lo, -jnp.inf))
    cand = jnp.concatenate(parts, axis=0)
    cur = cand
    tau = None
    for _ in range(PEER_TOPK):
        tau = jnp.max(cur, axis=0, keepdims=True)
        cur = jnp.where(cur == tau, -jnp.inf, cur)
    top = c1[0] + c2[0]
    z = jnp.sum(jnp.where(cand >= tau, jnp.exp(cand - top), 0.0), axis=0, keepdims=True)
    return tau, z


def _select(s1, s2):
    c1, _ = _top16(s1, False)
    c2, rank2 = _top16(s2, True)
    tau, z = _pair_threshold(c1, c2)
    c2s = jnp.concatenate(c2, axis=0)
    count = jnp.zeros(s1.shape, F32)
    for i in range(PEER_TOPK):
        n = jnp.sum(jnp.where(c1[i] + c2s >= tau, 1.0, 0.0), axis=0, keepdims=True)
        count = jnp.where(s1 == c1[i], n, count)
    e1 = jnp.exp(s1 - c1[0]) / z
    e2 = jnp.exp(s2 - c2[0])
    return rank2.astype(BF16), e2.astype(BF16), _dup_bf16(count), _dup_bf16(e1)


def _peer_kernel(h_ref, wq_ref, sk_ref, u_ref, vt_ref, g_ref, b_ref, o_ref,
                 hb_ref, qp_ref, s_ref, r2_ref, e2_ref, n1_ref, e1_ref,
                 act_a, act_b, wact_a, wact_b, acc_ref, *, tT, E):
    j = pl.program_id(1)
    n_steps = pl.num_programs(1)
    nl = tT // LANES
    na = E // PEER_NKEYS
    n_blk = PEER_N // E
    RB = 16

    def phase_a(act, half):
        act[...] = lax.dot_general(u_ref[half * E:(half + 1) * E, :], hb_ref[...], _NT,
                                   preferred_element_type=F32)

    @pl.when(j == 0)
    def _():
        hb = h_ref[...].astype(BF16)
        hb_ref[...] = hb
        QC = 512
        for c in range(PEER_HEADS * PEER_DQ // QC):
            qp_ref[c * QC:(c + 1) * QC, :] = lax.dot_general(
                wq_ref[c * QC:(c + 1) * QC, :], hb, _NT, preferred_element_type=F32).astype(BF16)
        half = PEER_DQ // 2
        for hp in range(2 * PEER_HEADS):
            sT = jnp.dot(sk_ref[hp], qp_ref[hp * half:(hp + 1) * half, :], preferred_element_type=F32)
            for l in range(nl):
                s_ref[hp, l] = sT[:, l * LANES:(l + 1) * LANES]

        def select(idx, carry):
            h = idx // nl
            l = idx % nl
            r2_ref[h, l], e2_ref[h, l], n1_ref[h, l], e1_ref[h, l] = _select(s_ref[2 * h, l], s_ref[2 * h + 1, l])
            return carry

        lax.fori_loop(0, PEER_HEADS * nl, select, 0)
        phase_a(act_a, 0)
        acc_ref[...] = jnp.zeros_like(acc_ref)

    def phase_v(act, wact, blk):
        blk = jnp.clip(blk, 0, n_blk - 1)

        def row_bf16(ref, h, l, a):
            return pltpu.bitcast(jnp.broadcast_to(ref[h, l, pl.ds(a, 1), :], (8, LANES)), BF16)

        for r in range(PEER_NKEYS // RB):
            b0 = r * RB
            for l in range(nl):
                lanes = slice(l * LANES, (l + 1) * LANES)
                w = [jnp.zeros((RB, LANES), BF16)] * na
                for h in range(PEER_HEADS):
                    r2c = r2_ref[h, l, b0:b0 + RB, :]
                    e2c = e2_ref[h, l, b0:b0 + RB, :]
                    for aa in range(na):
                        a = blk * na + aa
                        sel = r2c < row_bf16(n1_ref, h, l, a)
                        w[aa] = w[aa] + jnp.where(sel, row_bf16(e1_ref, h, l, a) * e2c, jnp.zeros((), BF16))
                for aa in range(na):
                    rows = slice(aa * PEER_NKEYS + b0, aa * PEER_NKEYS + b0 + RB)
                    x = act[rows, lanes]
                    gelu = 0.5 * x * (1.0 + lax.erf(x * SQRT_HALF))
                    wact[rows, lanes] = w[aa] * gelu.astype(BF16)

    def phase_c(wact, half):
        acc_ref[...] += jnp.dot(vt_ref[:, half * E:(half + 1) * E], wact[...], preferred_element_type=F32)

    @pl.when(j >= 1)
    def _():
        phase_c(wact_a, 0)
        phase_v(act_b, wact_b, 2 * j - 1)
        phase_c(wact_b, 1)
        phase_a(act_a, 0)

    phase_v(act_a, wact_a, 2 * j)
    phase_a(act_b, 1)

    @pl.when(j == n_steps - 1)
    def _():
        y = DEEPNORM_ALPHA * h_ref[...] + acc_ref[...].T
        o_ref[...] = _ln(y, g_ref[...], b_ref[...])


def _peer_ln(h, wq_t, sk, u, v_t, g, b, tT, E):
    T = h.shape[0]
    nl = tT // LANES
    n_pairs = PEER_N // (2 * E)
    row = lambda i, j: (i, 0)
    full2 = lambda i, j: (0, 0)
    head_shape = (PEER_HEADS, nl, PEER_NKEYS, LANES)
    return pl.pallas_call(
        functools.partial(_peer_kernel, tT=tT, E=E),
        grid=(T // tT, n_pairs + 1),
        in_specs=[pl.BlockSpec((tT, D_MODEL), row),
                  pl.BlockSpec((PEER_HEADS * PEER_DQ, D_MODEL), full2, pipeline_mode=pl.Buffered(1)),
                  pl.BlockSpec((2 * PEER_HEADS, PEER_NKEYS, PEER_DQ // 2), lambda i, j: (0, 0, 0)),
                  pl.BlockSpec((2 * E, D_MODEL), lambda i, j: (jnp.minimum(j, n_pairs - 1), 0)),
                  pl.BlockSpec((D_MODEL, 2 * E), lambda i, j: (0, jnp.maximum(j - 1, 0))),
                  pl.BlockSpec((1, D_MODEL), full2),
                  pl.BlockSpec((1, D_MODEL), full2)],
        out_specs=pl.BlockSpec((tT, D_MODEL), row),
        out_shape=jax.ShapeDtypeStruct((T, D_MODEL), F32),
        scratch_shapes=[pltpu.VMEM((tT, D_MODEL), BF16),
                        pltpu.VMEM((PEER_HEADS * PEER_DQ, tT), BF16),
                        pltpu.VMEM((2 * PEER_HEADS, nl, PEER_NKEYS, LANES), F32),
                        pltpu.VMEM(head_shape, BF16),
                        pltpu.VMEM(head_shape, BF16),
                        pltpu.VMEM(head_shape, jnp.uint32),
                        pltpu.VMEM(head_shape, jnp.uint32),
                        pltpu.VMEM((E, tT), F32), pltpu.VMEM((E, tT), F32),
                        pltpu.VMEM((E, tT), BF16), pltpu.VMEM((E, tT), BF16),
                        pltpu.VMEM((D_MODEL, tT), F32)],
        compiler_params=pltpu.CompilerParams(
            dimension_semantics=("parallel", "arbitrary"), vmem_limit_bytes=VMEM_LIMIT),
        name="peer_ln",
    )(h, wq_t, sk, u, v_t, g.reshape(1, -1), b.reshape(1, -1))


def _tile(n, pref):
    return pref if n % pref == 0 else n


def _run_layer(l, h, Bn, S, prefix, params, tabs, lam, last_meta=False):
    (w_in, conv_w, subln_g, w_out, ln1_g, ln1_b, wq_t, sk, u, v_t, ln2_g, ln2_b) = params
    lam_init = 0.8 - 0.6 * math.exp(-0.3 * l)
    is_meta = prefix is None
    t = _tile(S, 256)
    ts = _tile(S, 512)
    halo = jnp.zeros((8, CONV_WIDTH), F32) if is_meta else prefix[2]
    outs = _proj_conv(h, w_in, conv_w, halo, Bn, S, ts, emit_u=is_meta)
    q, k, v, c = outs[:4]
    new_prefix = (k, v, outs[4][N_META - 8:N_META]) if is_meta else None
    if last_meta:
        return None, new_prefix
    kp, vp = (k, v) if is_meta else prefix[:2]
    a = _diff_attention(q, k, v, kp, vp, tabs[0], tabs[1], lam, subln_g, Bn, S, t,
                        has_prefix=not is_meta, lam_init=lam_init)
    T = Bn * S
    h1 = _outproj_ln(a, c, h, w_out, ln1_g, ln1_b, _tile(T, 512))
    h2 = _peer_ln(h1, wq_t, sk, u, v_t, ln2_g, ln2_b, _tile(T, 512), 512)
    return h2, new_prefix


def kernel(x, meta_tokens, ln_in_g, ln_in_b, rel_bias, w_in, conv_w, lambda_q1, lambda_k1, lambda_q2,
           lambda_k2, subln_g, w_out, ln1_g, ln1_b, peer_w_q, peer_sub_keys, peer_u, peer_v, ln2_g, ln2_b):
    Bn, S, _ = x.shape
    depth = w_in.shape[0]
    w_in_b = w_in.astype(BF16)
    w_out_b = w_out.astype(BF16)
    wq_t = jnp.swapaxes(peer_w_q, 1, 2).astype(BF16)
    sk = peer_sub_keys.reshape(depth, 2 * PEER_HEADS, PEER_NKEYS, PEER_DQ // 2).astype(BF16)
    u_b = peer_u.astype(BF16)
    v_t = jnp.swapaxes(peer_v, 1, 2).astype(BF16)

    t_main = _tile(S, 256)
    tabs_main = _bias_tables(rel_bias, t_main)
    tabs_meta = _bias_tables(rel_bias, META_ROWS)

    meta = jnp.zeros((META_ROWS, D_MODEL), F32).at[:N_META].set(meta_tokens.astype(F32))
    hm = _layer_norm_rows(meta, ln_in_g, ln_in_b, META_ROWS)
    h = _layer_norm_rows(x.reshape(Bn * S, D_MODEL), ln_in_g, ln_in_b, _tile(Bn * S, 512))

    for l in range(depth):
        lam_init = 0.8 - 0.6 * math.exp(-0.3 * l)
        lam = _lambda(lambda_q1[l], lambda_k1[l], lambda_q2[l], lambda_k2[l], lam_init)
        params = (w_in_b[l], conv_w[l], subln_g[l], w_out_b[l], ln1_g[l], ln1_b[l], wq_t[l], sk[l],
                  u_b[l], v_t[l], ln2_g[l], ln2_b[l])
        hm, prefix = _run_layer(l, hm, 1, META_ROWS, None, params, tabs_meta, lam,
                                last_meta=(l == depth - 1))
        h, _ = _run_layer(l, h, Bn, S, prefix, params, tabs_main, lam)
    return h.reshape(Bn, S, D_MODEL)
```

```python
import functools
import math

import jax
import jax.numpy as jnp
from jax import lax
from jax.experimental import pallas as pl
from jax.experimental.pallas import tpu as pltpu

F32 = jnp.float32
BF16 = jnp.bfloat16

D_MODEL = 1024
DEPTH = 4
N_META = 16
META_ROWS = 128
ATTN_HEADS = 4
ATTN_DK = 64
ATTN_DV = 2 * ATTN_DK
ATTN_WIDTH = ATTN_HEADS * ATTN_DV
CONV_WIDTH = D_MODEL - ATTN_WIDTH
CONV_K = 3
IN_COLS = 3 * ATTN_WIDTH + 3 * CONV_WIDTH
REL_BUCKETS = 32
REL_MAX_DIST = 128
PEER_HEADS = 8
PEER_NKEYS = 128
PEER_N = PEER_NKEYS * PEER_NKEYS
PEER_DQ = 256
PEER_TOPK = 16
DEEPNORM_ALPHA = (2 * DEPTH) ** 0.25
LN_EPS = 1e-5
RMS_EPS = 1e-5
NEG_BIG = -1e30
SQRT_HALF = math.sqrt(0.5)

LANES = 128
VMEM_LIMIT = 56 * 1024 * 1024

_NT = (((1,), (1,)), ((), ()))


def _ln(y, g, b):
    mu = jnp.mean(y, axis=-1, keepdims=True)
    yc = y - mu
    var = jnp.mean(yc * yc, axis=-1, keepdims=True)
    return yc * lax.rsqrt(var + LN_EPS) * g + b


def _ln_kernel(x_ref, g_ref, b_ref, o_ref):
    o_ref[...] = _ln(x_ref[...], g_ref[...], b_ref[...])


def _layer_norm_rows(x, g, b, tm):
    T = x.shape[0]
    return pl.pallas_call(
        _ln_kernel,
        grid=(T // tm,),
        in_specs=[pl.BlockSpec((tm, D_MODEL), lambda i: (i, 0)),
                  pl.BlockSpec((1, D_MODEL), lambda i: (0, 0)),
                  pl.BlockSpec((1, D_MODEL), lambda i: (0, 0))],
        out_specs=pl.BlockSpec((tm, D_MODEL), lambda i: (i, 0)),
        out_shape=jax.ShapeDtypeStruct((T, D_MODEL), F32),
        compiler_params=pltpu.CompilerParams(dimension_semantics=("parallel",)),
        name="ln_in",
    )(x, g.reshape(1, -1), b.reshape(1, -1))


def _proj_kernel(h_ref, w_ref, cw_ref, halo_ref, q_ref, k_ref, v_ref, c_ref, *rest, ts, emit_u):
    if emit_u:
        u_ref, ubuf = rest
    else:
        (ubuf,) = rest
    W = ATTN_WIDTH

    @pl.when(pl.program_id(1) == 0)
    def _():
        ubuf[0:8, :] = halo_ref[...]

    hb = h_ref[...].astype(BF16)

    def col(c):
        return jnp.dot(hb, w_ref[:, c * W:(c + 1) * W], preferred_element_type=F32)

    q_ref[...] = (col(0) * (ATTN_DK ** -0.5)).astype(BF16)
    k_ref[...] = col(1).astype(BF16)
    v_ref[...] = col(2).astype(BF16)
    u = col(4) * col(5)
    ubuf[8:8 + ts, :] = u
    if emit_u:
        u_ref[...] = u
    conv = (cw_ref[0:1, :] * ubuf[6:6 + ts, :] + cw_ref[1:2, :] * ubuf[7:7 + ts, :]
            + cw_ref[2:3, :] * u)
    c_ref[...] = (col(3) * conv).astype(BF16)
    ubuf[0:8, :] = ubuf[ts:ts + 8, :]


def _proj_conv(h, w_in, conv_w, halo, Bn, S, ts, emit_u):
    T = Bn * S
    nS = S // ts
    row = lambda b, s: (b * nS + s, 0)
    full = lambda b, s: (0, 0)
    out_shape = [jax.ShapeDtypeStruct((T, ATTN_WIDTH), BF16)] * 4
    out_specs = [pl.BlockSpec((ts, ATTN_WIDTH), row)] * 4
    if emit_u:
        out_shape = out_shape + [jax.ShapeDtypeStruct((T, CONV_WIDTH), F32)]
        out_specs = out_specs + [pl.BlockSpec((ts, CONV_WIDTH), row)]
    return pl.pallas_call(
        functools.partial(_proj_kernel, ts=ts, emit_u=emit_u),
        grid=(Bn, nS),
        in_specs=[pl.BlockSpec((ts, D_MODEL), row),
                  pl.BlockSpec((D_MODEL, IN_COLS), full),
                  pl.BlockSpec((CONV_K, CONV_WIDTH), full),
                  pl.BlockSpec((8, CONV_WIDTH), full)],
        out_specs=out_specs,
        out_shape=out_shape,
        scratch_shapes=[pltpu.VMEM((ts + 8, CONV_WIDTH), F32)],
        compiler_params=pltpu.CompilerParams(
            dimension_semantics=("parallel", "arbitrary"), vmem_limit_bytes=VMEM_LIMIT),
        name="proj_conv",
    )(h, w_in, conv_w, halo)


def _bias_kernel(rb_ref, d_ref, p_ref, *, t):
    h = pl.program_id(0)
    far = rb_ref[REL_BUCKETS - 1, h]

    def table(n):
        max_exact = REL_BUCKETS // 2
        nf = jnp.maximum(n, 1).astype(F32)
        large = max_exact + (jnp.log(nf / max_exact) / math.log(REL_MAX_DIST / max_exact)
                             * (REL_BUCKETS - max_exact)).astype(jnp.int32)
        large = jnp.minimum(large, REL_BUCKETS - 1)
        bucket = jnp.where(n < max_exact, n, large)
        out = jnp.zeros(n.shape, F32)
        for bkt in range(REL_BUCKETS):
            out = jnp.where(bucket == bkt, rb_ref[bkt, h], out)
        return out - far

    r = lax.broadcasted_iota(jnp.int32, (t, t), 0)
    c = lax.broadcasted_iota(jnp.int32, (t, t), 1)
    d_ref[0, 0] = jnp.where(c <= r, table(jnp.maximum(r - c, 0)), NEG_BIG)
    d_ref[0, 1] = table(t + r - c)
    rp = lax.broadcasted_iota(jnp.int32, (t, LANES), 0)
    cp = lax.broadcasted_iota(jnp.int32, (t, LANES), 1)
    p_ref[0, 0] = jnp.where(cp < N_META, table(N_META + rp - cp), NEG_BIG)
    p_ref[0, 1] = jnp.where(cp < N_META, 0.0, NEG_BIG)


def _bias_tables(rel_bias, t):
    return pl.pallas_call(
        functools.partial(_bias_kernel, t=t),
        grid=(ATTN_HEADS,),
        in_specs=[pl.BlockSpec(memory_space=pltpu.SMEM)],
        out_specs=[pl.BlockSpec((1, 2, t, t), lambda h: (h, 0, 0, 0)),
                   pl.BlockSpec((1, 2, t, LANES), lambda h: (h, 0, 0, 0))],
        out_shape=[jax.ShapeDtypeStruct((ATTN_HEADS, 2, t, t), F32),
                   jax.ShapeDtypeStruct((ATTN_HEADS, 2, t, LANES), F32)],
        name="bias_tables",
    )(rel_bias)


def _attn_kernel(q_ref, k_ref, v_ref, kp_ref, vp_ref, d_ref, p_ref, lam_ref, g_ref, o_ref,
                 s_ref, sp_ref, m_ref, l_ref, acc_ref, *, t, has_prefix, lam_init):
    i = pl.program_id(1)
    nc = t // LANES
    hm = [(h, m) for h in range(ATTN_HEADS) for m in range(2)]

    def cols(h, m):
        c0 = h * ATTN_DV + m * ATTN_DK
        return slice(c0, c0 + ATTN_DK)

    def vcols(h):
        return slice(h * ATTN_DV, (h + 1) * ATTN_DV)

    def lane_fold(x, op):
        out = x[:, :LANES]
        for c in range(1, x.shape[1] // LANES):
            out = op(out, x[:, c * LANES:(c + 1) * LANES])
        return out

    def logits(h, m, k_tile_ref, rows):
        return lax.dot_general(q_ref[:, cols(h, m)], k_tile_ref[rows, cols(h, m)], _NT,
                               preferred_element_type=F32)

    if has_prefix:
        first = jnp.minimum(i, 1)
        for h, m in hm:
            sp = logits(h, m, kp_ref, slice(None)) + p_ref[h, first]
            sp_ref[h, m] = sp
            m_ref[h, m] = sp
    else:
        m_ref[...] = jnp.full(m_ref.shape, -jnp.inf, F32)

    def score_tile(j, near):
        rows = pl.ds(pl.multiple_of(j * t, t), t)
        for h, m in hm:
            s = logits(h, m, k_ref, rows)
            if near:
                s = s + d_ref[h, i - j]
            s_ref[h, m, j] = s
            m_ref[h, m] = jnp.maximum(m_ref[h, m], lane_fold(s, jnp.maximum))

    def far_body(j, c):
        score_tile(j, False)
        return c

    def near_body(j, c):
        score_tile(j, True)
        return c

    lax.fori_loop(0, jnp.maximum(i - 1, 0), far_body, 0)
    lax.fori_loop(jnp.maximum(i - 1, 0), i + 1, near_body, 0)

    for h, m in hm:
        m_ref[h, m] = jnp.broadcast_to(jnp.max(m_ref[h, m], axis=-1, keepdims=True), (t, LANES))

    if has_prefix:
        for h, m in hm:
            p = jnp.exp(sp_ref[h, m] - m_ref[h, m])
            l_ref[h, m] = p
            acc_ref[h, m] = jnp.dot(p.astype(BF16), vp_ref[:, vcols(h)], preferred_element_type=F32)
    else:
        l_ref[...] = jnp.zeros_like(l_ref)
        acc_ref[...] = jnp.zeros_like(acc_ref)

    def value_body(j, c):
        rows = pl.ds(pl.multiple_of(j * t, t), t)
        for h, m in hm:
            p = jnp.exp(s_ref[h, m, j] - jnp.concatenate([m_ref[h, m]] * nc, axis=1))
            l_ref[h, m] += lane_fold(p, jnp.add)
            acc_ref[h, m] += jnp.dot(p.astype(BF16), v_ref[rows, vcols(h)], preferred_element_type=F32)
        return c

    lax.fori_loop(0, i + 1, value_body, 0)

    lam = lam_ref[0, 0]
    for h in range(ATTN_HEADS):
        o = [acc_ref[h, m] / jnp.sum(l_ref[h, m], axis=-1, keepdims=True) for m in range(2)]
        out = o[0] - lam * o[1]
        out = out * lax.rsqrt(jnp.mean(out * out, axis=-1, keepdims=True) + RMS_EPS)
        o_ref[:, vcols(h)] = (out * g_ref[...] * (1.0 - lam_init)).astype(BF16)


def _diff_attention(q, k, v, kp, vp, dtab, ptab, lam, subln_g, Bn, S, t, has_prefix, lam_init):
    nq = S // t
    W = ATTN_WIDTH
    hm_shape = (ATTN_HEADS, 2, t, LANES)
    return pl.pallas_call(
        functools.partial(_attn_kernel, t=t, has_prefix=has_prefix, lam_init=lam_init),
        grid=(Bn, nq),
        in_specs=[pl.BlockSpec((t, W), lambda b, i: (b * nq + i, 0)),
                  pl.BlockSpec((S, W), lambda b, i: (b, 0)),
                  pl.BlockSpec((S, W), lambda b, i: (b, 0)),
                  pl.BlockSpec((META_ROWS, W), lambda b, i: (0, 0)),
                  pl.BlockSpec((META_ROWS, W), lambda b, i: (0, 0)),
                  pl.BlockSpec((ATTN_HEADS, 2, t, t), lambda b, i: (0, 0, 0, 0)),
                  pl.BlockSpec((ATTN_HEADS, 2, t, LANES), lambda b, i: (0, 0, 0, 0)),
                  pl.BlockSpec(memory_space=pltpu.SMEM),
                  pl.BlockSpec((1, ATTN_DV), lambda b, i: (0, 0))],
        out_specs=pl.BlockSpec((t, W), lambda b, i: (b * nq + i, 0)),
        out_shape=jax.ShapeDtypeStruct((Bn * S, W), BF16),
        scratch_shapes=[pltpu.VMEM((ATTN_HEADS, 2, nq, t, t), F32),
                        pltpu.VMEM(hm_shape, F32),
                        pltpu.VMEM(hm_shape, F32),
                        pltpu.VMEM(hm_shape, F32),
                        pltpu.VMEM(hm_shape, F32)],
        compiler_params=pltpu.CompilerParams(
            dimension_semantics=("parallel", "arbitrary"), vmem_limit_bytes=VMEM_LIMIT),
        name="diff_attn",
    )(q, k, v, kp, vp, dtab, ptab, lam, subln_g.reshape(1, -1))


def _lam_kernel(q1_ref, k1_ref, q2_ref, k2_ref, o_ref, *, lam_init):
    a = jnp.sum(q1_ref[...] * k1_ref[...], axis=-1, keepdims=True)
    b = jnp.sum(q2_ref[...] * k2_ref[...], axis=-1, keepdims=True)
    o_ref[...] = jnp.broadcast_to(jnp.exp(a) - jnp.exp(b) + lam_init, o_ref.shape)


def _lambda(q1, k1, q2, k2, lam_init):
    r = lambda x: x.reshape(1, ATTN_DK)
    return pl.pallas_call(
        functools.partial(_lam_kernel, lam_init=lam_init),
        out_shape=jax.ShapeDtypeStruct((1, LANES), F32),
        name="lambda",
    )(r(q1), r(k1), r(q2), r(k2))


def _outproj_kernel(a_ref, c_ref, h_ref, w_ref, g_ref, b_ref, o_ref):
    mix = jnp.dot(a_ref[...], w_ref[0:ATTN_WIDTH, :], preferred_element_type=F32)
    mix = mix + jnp.dot(c_ref[...], w_ref[ATTN_WIDTH:, :], preferred_element_type=F32)
    o_ref[...] = _ln(DEEPNORM_ALPHA * h_ref[...] + mix, g_ref[...], b_ref[...])


def _outproj_ln(a, c, h, w_out, g, b, tm):
    T = h.shape[0]
    row = lambda i: (i, 0)
    full = lambda i: (0, 0)
    return pl.pallas_call(
        _outproj_kernel,
        grid=(T // tm,),
        in_specs=[pl.BlockSpec((tm, ATTN_WIDTH), row),
                  pl.BlockSpec((tm, CONV_WIDTH), row),
                  pl.BlockSpec((tm, D_MODEL), row),
                  pl.BlockSpec((D_MODEL, D_MODEL), full),
                  pl.BlockSpec((1, D_MODEL), full),
                  pl.BlockSpec((1, D_MODEL), full)],
        out_specs=pl.BlockSpec((tm, D_MODEL), row),
        out_shape=jax.ShapeDtypeStruct((T, D_MODEL), F32),
        compiler_params=pltpu.CompilerParams(
            dimension_semantics=("parallel",), vmem_limit_bytes=VMEM_LIMIT),
        name="outproj_ln",
    )(a, c, h, w_out, g.reshape(1, -1), b.reshape(1, -1))


NOT_SELECTED = 64.0


def _top16(s, with_rank):
    vals = []
    cur = s
    rank = jnp.full(s.shape, NOT_SELECTED, F32) if with_rank else None
    for i in range(PEER_TOPK):
        m = jnp.max(cur, axis=0, keepdims=True)
        vals.append(m)
        hit = cur == m
        if with_rank:
            rank = jnp.where(hit, float(i), rank)
        cur = jnp.where(hit, -jnp.inf, cur)
    return vals, rank


def _dup_bf16(x):
    b = pltpu.bitcast(x, jnp.uint32)
    hi = (b + jnp.uint32(0x7FFF) + ((b >> 16) & jnp.uint32(1))) & jnp.uint32(0xFFFF0000)
    return hi | (hi >> 16)


def _pair_threshold(c1, c2):
    c1s = jnp.concatenate(c1, axis=0)
    c2s = jnp.concatenate(c2, axis=0)
    lo = c2s[0:8, :]
    row = lax.broadcasted_iota(jnp.int32, lo.shape, 0)
    parts = [c1s + c2[0], c1s[0:8, :] + c2[1], c1s[0:8, :] + c2[2], c1s[0:8, :] + c2[3],
             c1[0] + c2s[8:16, :]]
    for i in range(3):
        parts.append(jnp.where(row >= 4, c1[i] + lo, -jnp.inf))
    cand = jnp.concatenate(parts, axis=0)
    cur = cand
    tau = None
    for _ in range(PEER_TOPK):
        tau = jnp.max(cur, axis=0, keepdims=True)
        cur = jnp.where(cur == tau, -jnp.inf, cur)
    top = c1[0] + c2[0]
    z = jnp.sum(jnp.where(cand >= tau, jnp.exp(cand - top), 0.0), axis=0, keepdims=True)
    return tau, z


def _select(s1, s2):
    c1, _ = _top16(s1, False)
    c2, rank2 = _top16(s2, True)
    tau, z = _pair_threshold(c1, c2)
    c2s = jnp.concatenate(c2, axis=0)
    count = jnp.zeros(s1.shape, F32)
    for i in range(PEER_TOPK):
        n = jnp.sum(jnp.where(c1[i] + c2s >= tau, 1.0, 0.0), axis=0, keepdims=True)
        count = jnp.where(s1 == c1[i], n, count)
    e1 = jnp.exp(s1 - c1[0]) / z
    e2 = jnp.exp(s2 - c2[0])
    return rank2.astype(BF16), e2.astype(BF16), _dup_bf16(count), _dup_bf16(e1)


def _peer_kernel(h_ref, wq_ref, sk_ref, u_ref, vt_ref, g_ref, b_ref, o_ref,
                 hb_ref, qp_ref, s_ref, r2_ref, e2_ref, n1_ref, e1_ref,
                 act_a, act_b, wact_a, wact_b, acc_ref, *, tT, E):
    j = pl.program_id(1)
    n_steps = pl.num_programs(1)
    nl = tT // LANES
    na = E // PEER_NKEYS
    n_blk = PEER_N // E
    RB = 16

    def phase_a(act, half):
        act[...] = lax.dot_general(u_ref[half * E:(half + 1) * E, :], hb_ref[...], _NT,
                                   preferred_element_type=F32)

    @pl.when(j == 0)
    def _():
        hb = h_ref[...].astype(BF16)
        hb_ref[...] = hb
        QC = 512
        for c in range(PEER_HEADS * PEER_DQ // QC):
            qp_ref[c * QC:(c + 1) * QC, :] = lax.dot_general(
                wq_ref[c * QC:(c + 1) * QC, :], hb, _NT, preferred_element_type=F32).astype(BF16)
        half = PEER_DQ // 2
        for hp in range(2 * PEER_HEADS):
            sT = jnp.dot(sk_ref[hp], qp_ref[hp * half:(hp + 1) * half, :], preferred_element_type=F32)
            for l in range(nl):
                s_ref[hp, l] = sT[:, l * LANES:(l + 1) * LANES]

        def select(idx, carry):
            h = idx // nl
            l = idx % nl
            r2_ref[h, l], e2_ref[h, l], n1_ref[h, l], e1_ref[h, l] = _select(s_ref[2 * h, l], s_ref[2 * h + 1, l])
            return carry

        lax.fori_loop(0, PEER_HEADS * nl, select, 0)
        phase_a(act_a, 0)
        acc_ref[...] = jnp.zeros_like(acc_ref)

    def phase_v(act, wact, blk):
        blk = jnp.clip(blk, 0, n_blk - 1)

        def row_bf16(ref, h, l, a):
            return pltpu.bitcast(jnp.broadcast_to(ref[h, l, pl.ds(a, 1), :], (8, LANES)), BF16)

        for r in range(PEER_NKEYS // RB):
            b0 = r * RB
            for l in range(nl):
                lanes = slice(l * LANES, (l + 1) * LANES)
                w = [jnp.zeros((RB, LANES), BF16)] * na
                for h in range(PEER_HEADS):
                    r2c = r2_ref[h, l, b0:b0 + RB, :]
                    e2c = e2_ref[h, l, b0:b0 + RB, :]
                    for aa in range(na):
                        a = blk * na + aa
                        sel = r2c < row_bf16(n1_ref, h, l, a)
                        w[aa] = w[aa] + jnp.where(sel, row_bf16(e1_ref, h, l, a) * e2c, jnp.zeros((), BF16))
                for aa in range(na):
                    rows = slice(aa * PEER_NKEYS + b0, aa * PEER_NKEYS + b0 + RB)
                    x = act[rows, lanes]
                    gelu = 0.5 * x * (1.0 + lax.erf(x * SQRT_HALF))
                    wact[rows, lanes] = w[aa] * gelu.astype(BF16)

    def phase_c(wact, half):
        acc_ref[...] += jnp.dot(vt_ref[:, half * E:(half + 1) * E], wact[...], preferred_element_type=F32)

    @pl.when(j >= 1)
    def _():
        phase_c(wact_a, 0)
        phase_v(act_b, wact_b, 2 * j - 1)
        phase_c(wact_b, 1)
        phase_a(act_a, 0)

    phase_v(act_a, wact_a, 2 * j)
    phase_a(act_b, 1)

    @pl.when(j == n_steps - 1)
    def _():
        y = DEEPNORM_ALPHA * h_ref[...] + acc_ref[...].T
        o_ref[...] = _ln(y, g_ref[...], b_ref[...])


def _peer_ln(h, wq_t, sk, u, v_t, g, b, tT, E):
    T = h.shape[0]
    nl = tT // LANES
    n_pairs = PEER_N // (2 * E)
    row = lambda i, j: (i, 0)
    full2 = lambda i, j: (0, 0)
    head_shape = (PEER_HEADS, nl, PEER_NKEYS, LANES)
    return pl.pallas_call(
        functools.partial(_peer_kernel, tT=tT, E=E),
        grid=(T // tT, n_pairs + 1),
        in_specs=[pl.BlockSpec((tT, D_MODEL), row),
                  pl.BlockSpec((PEER_HEADS * PEER_DQ, D_MODEL), full2, pipeline_mode=pl.Buffered(1)),
                  pl.BlockSpec((2 * PEER_HEADS, PEER_NKEYS, PEER_DQ // 2), lambda i, j: (0, 0, 0)),
                  pl.BlockSpec((2 * E, D_MODEL), lambda i, j: (jnp.minimum(j, n_pairs - 1), 0)),
                  pl.BlockSpec((D_MODEL, 2 * E), lambda i, j: (0, jnp.maximum(j - 1, 0))),
                  pl.BlockSpec((1, D_MODEL), full2),
                  pl.BlockSpec((1, D_MODEL), full2)],
        out_specs=pl.BlockSpec((tT, D_MODEL), row),
        out_shape=jax.ShapeDtypeStruct((T, D_MODEL), F32),
        scratch_shapes=[pltpu.VMEM((tT, D_MODEL), BF16),
                        pltpu.VMEM((PEER_HEADS * PEER_DQ, tT), BF16),
                        pltpu.VMEM((2 * PEER_HEADS, nl, PEER_NKEYS, LANES), F32),
                        pltpu.VMEM(head_shape, BF16),
                        pltpu.VMEM(head_shape, BF16),
                        pltpu.VMEM(head_shape, jnp.uint32),
                        pltpu.VMEM(head_shape, jnp.uint32),
                        pltpu.VMEM((E, tT), F32), pltpu.VMEM((E, tT), F32),
                        pltpu.VMEM((E, tT), BF16), pltpu.VMEM((E, tT), BF16),
                        pltpu.VMEM((D_MODEL, tT), F32)],
        compiler_params=pltpu.CompilerParams(
            dimension_semantics=("parallel", "arbitrary"), vmem_limit_bytes=VMEM_LIMIT),
        name="peer_ln",
    )(h, wq_t, sk, u, v_t, g.reshape(1, -1), b.reshape(1, -1))


def _tile(n, pref):
    return pref if n % pref == 0 else n


def _run_layer(l, h, Bn, S, prefix, params, tabs, lam, last_meta=False):
    (w_in, conv_w, subln_g, w_out, ln1_g, ln1_b, wq_t, sk, u, v_t, ln2_g, ln2_b) = params
    lam_init = 0.8 - 0.6 * math.exp(-0.3 * l)
    is_meta = prefix is None
    t = _tile(S, 256)
    ts = _tile(S, 512)
    halo = jnp.zeros((8, CONV_WIDTH), F32) if is_meta else prefix[2]
    outs = _proj_conv(h, w_in, conv_w, halo, Bn, S, ts, emit_u=is_meta)
    q, k, v, c = outs[:4]
    new_prefix = (k, v, outs[4][N_META - 8:N_META]) if is_meta else None
    if last_meta:
        return None, new_prefix
    kp, vp = (k, v) if is_meta else prefix[:2]
    a = _diff_attention(q, k, v, kp, vp, tabs[0], tabs[1], lam, subln_g, Bn, S, t,
                        has_prefix=not is_meta, lam_init=lam_init)
    T = Bn * S
    h1 = _outproj_ln(a, c, h, w_out, ln1_g, ln1_b, _tile(T, 512))
    h2 = _peer_ln(h1, wq_t, sk, u, v_t, ln2_g, ln2_b, _tile(T, 512), 512)
    return h2, new_prefix


def kernel(x, meta_tokens, ln_in_g, ln_in_b, rel_bias, w_in, conv_w, lambda_q1, lambda_k1, lambda_q2,
           lambda_k2, subln_g, w_out, ln1_g, ln1_b, peer_w_q, peer_sub_keys, peer_u, peer_v, ln2_g, ln2_b):
    Bn, S, _ = x.shape
    depth = w_in.shape[0]
    w_in_b = w_in.astype(BF16)
    w_out_b = w_out.astype(BF16)
    wq_t = jnp.swapaxes(peer_w_q, 1, 2).astype(BF16)
    sk = peer_sub_keys.reshape(depth, 2 * PEER_HEADS, PEER_NKEYS, PEER_DQ // 2).astype(BF16)
    u_b = peer_u.astype(BF16)
    v_t = jnp.swapaxes(peer_v, 1, 2).astype(BF16)

    t_main = _tile(S, 256)
    tabs_main = _bias_tables(rel_bias, t_main)
    tabs_meta = _bias_tables(rel_bias, META_ROWS)

    meta = jnp.zeros((META_ROWS, D_MODEL), F32).at[:N_META].set(meta_tokens.astype(F32))
    hm = _layer_norm_rows(meta, ln_in_g, ln_in_b, META_ROWS)
    h = _layer_norm_rows(x.reshape(Bn * S, D_MODEL), ln_in_g, ln_in_b, _tile(Bn * S, 512))

    for l in range(depth):
        lam_init = 0.8 - 0.6 * math.exp(-0.3 * l)
        lam = _lambda(lambda_q1[l], lambda_k1[l], lambda_q2[l], lambda_k2[l], lam_init)
        params = (w_in_b[l], conv_w[l], subln_g[l], w_out_b[l], ln1_g[l], ln1_b[l], wq_t[l], sk[l],
                  u_b[l], v_t[l], ln2_g[l], ln2_b[l])
        hm, prefix = _run_layer(l, hm, 1, META_ROWS, None, params, tabs_meta, lam,
                                last_meta=(l == depth - 1))
        h, _ = _run_layer(l, h, Bn, S, prefix, params, tabs_main, lam)
    return h.reshape(Bn, S, D_MODEL)
```

```python
import functools
import math

import jax
import jax.numpy as jnp
from jax import lax
from jax.experimental import pallas as pl
from jax.experimental.pallas import tpu as pltpu

F32 = jnp.float32
BF16 = jnp.bfloat16

D_MODEL = 1024
DEPTH = 4
N_META = 16
META_ROWS = 128
ATTN_HEADS = 4
ATTN_DK = 64
ATTN_DV = 2 * ATTN_DK
ATTN_WIDTH = ATTN_HEADS * ATTN_DV
CONV_WIDTH = D_MODEL - ATTN_WIDTH
CONV_K = 3
IN_COLS = 3 * ATTN_WIDTH + 3 * CONV_WIDTH
REL_BUCKETS = 32
REL_MAX_DIST = 128
PEER_HEADS = 8
PEER_NKEYS = 128
PEER_N = PEER_NKEYS * PEER_NKEYS
PEER_DQ = 256
PEER_TOPK = 16
DEEPNORM_ALPHA = (2 * DEPTH) ** 0.25
LN_EPS = 1e-5
RMS_EPS = 1e-5
NEG_BIG = -1e30
SQRT_HALF = math.sqrt(0.5)

LANES = 128
VMEM_LIMIT = 56 * 1024 * 1024

_NT = (((1,), (1,)), ((), ()))


def _ln(y, g, b):
    mu = jnp.mean(y, axis=-1, keepdims=True)
    yc = y - mu
    var = jnp.mean(yc * yc, axis=-1, keepdims=True)
    return yc * lax.rsqrt(var + LN_EPS) * g + b


def _ln_kernel(x_ref, g_ref, b_ref, o_ref):
    o_ref[...] = _ln(x_ref[...], g_ref[...], b_ref[...])


def _layer_norm_rows(x, g, b, tm):
    T = x.shape[0]
    return pl.pallas_call(
        _ln_kernel,
        grid=(T // tm,),
        in_specs=[pl.BlockSpec((tm, D_MODEL), lambda i: (i, 0)),
                  pl.BlockSpec((1, D_MODEL), lambda i: (0, 0)),
                  pl.BlockSpec((1, D_MODEL), lambda i: (0, 0))],
        out_specs=pl.BlockSpec((tm, D_MODEL), lambda i: (i, 0)),
        out_shape=jax.ShapeDtypeStruct((T, D_MODEL), F32),
        compiler_params=pltpu.CompilerParams(dimension_semantics=("parallel",)),
        name="ln_in",
    )(x, g.reshape(1, -1), b.reshape(1, -1))


def _proj_kernel(h_ref, w_ref, cw_ref, halo_ref, q_ref, k_ref, v_ref, c_ref, *rest, ts, emit_u):
    if emit_u:
        u_ref, ubuf = rest
    else:
        (ubuf,) = rest
    W = ATTN_WIDTH

    @pl.when(pl.program_id(1) == 0)
    def _():
        ubuf[0:8, :] = halo_ref[...]

    hb = h_ref[...].astype(BF16)

    def col(c):
        return jnp.dot(hb, w_ref[:, c * W:(c + 1) * W], preferred_element_type=F32)

    q_ref[...] = (col(0) * (ATTN_DK ** -0.5)).astype(BF16)
    k_ref[...] = col(1).astype(BF16)
    v_ref[...] = col(2).astype(BF16)
    u = col(4) * col(5)
    ubuf[8:8 + ts, :] = u
    if emit_u:
        u_ref[...] = u
    conv = (cw_ref[0:1, :] * ubuf[6:6 + ts, :] + cw_ref[1:2, :] * ubuf[7:7 + ts, :]
            + cw_ref[2:3, :] * u)
    c_ref[...] = (col(3) * conv).astype(BF16)
    ubuf[0:8, :] = ubuf[ts:ts + 8, :]


def _proj_conv(h, w_in, conv_w, halo, Bn, S, ts, emit_u):
    T = Bn * S
    nS = S // ts
    row = lambda b, s: (b * nS + s, 0)
    full = lambda b, s: (0, 0)
    out_shape = [jax.ShapeDtypeStruct((T, ATTN_WIDTH), BF16)] * 4
    out_specs = [pl.BlockSpec((ts, ATTN_WIDTH), row)] * 4
    if emit_u:
        out_shape = out_shape + [jax.ShapeDtypeStruct((T, CONV_WIDTH), F32)]
        out_specs = out_specs + [pl.BlockSpec((ts, CONV_WIDTH), row)]
    return pl.pallas_call(
        functools.partial(_proj_kernel, ts=ts, emit_u=emit_u),
        grid=(Bn, nS),
        in_specs=[pl.BlockSpec((ts, D_MODEL), row),
                  pl.BlockSpec((D_MODEL, IN_COLS), full),
                  pl.BlockSpec((CONV_K, CONV_WIDTH), full),
                  pl.BlockSpec((8, CONV_WIDTH), full)],
        out_specs=out_specs,
        out_shape=out_shape,
        scratch_shapes=[pltpu.VMEM((ts + 8, CONV_WIDTH), F32)],
        compiler_params=pltpu.CompilerParams(
            dimension_semantics=("parallel", "arbitrary"), vmem_limit_bytes=VMEM_LIMIT),
        name="proj_conv",
    )(h, w_in, conv_w, halo)


def _bias_kernel(rb_ref, d_ref, p_ref, *, t):
    h = pl.program_id(0)
    far = rb_ref[REL_BUCKETS - 1, h]

    def table(n):
        max_exact = REL_BUCKETS // 2
        nf = jnp.maximum(n, 1).astype(F32)
        large = max_exact + (jnp.log(nf / max_exact) / math.log(REL_MAX_DIST / max_exact)
                             * (REL_BUCKETS - max_exact)).astype(jnp.int32)
        large = jnp.minimum(large, REL_BUCKETS - 1)
        bucket = jnp.where(n < max_exact, n, large)
        out = jnp.zeros(n.shape, F32)
        for bkt in range(REL_BUCKETS):
            out = jnp.where(bucket == bkt, rb_ref[bkt, h], out)
        return out - far

    r = lax.broadcasted_iota(jnp.int32, (t, t), 0)
    c = lax.broadcasted_iota(jnp.int32, (t, t), 1)
    d_ref[0, 0] = jnp.where(c <= r, table(jnp.maximum(r - c, 0)), NEG_BIG)
    d_ref[0, 1] = table(t + r - c)
    rp = lax.broadcasted_iota(jnp.int32, (t, LANES), 0)
    cp = lax.broadcasted_iota(jnp.int32, (t, LANES), 1)
    p_ref[0, 0] = jnp.where(cp < N_META, table(N_META + rp - cp), NEG_BIG)
    p_ref[0, 1] = jnp.where(cp < N_META, 0.0, NEG_BIG)


def _bias_tables(rel_bias, t):
    return pl.pallas_call(
        functools.partial(_bias_kernel, t=t),
        grid=(ATTN_HEADS,),
        in_specs=[pl.BlockSpec(memory_space=pltpu.SMEM)],
        out_specs=[pl.BlockSpec((1, 2, t, t), lambda h: (h, 0, 0, 0)),
                   pl.BlockSpec((1, 2, t, LANES), lambda h: (h, 0, 0, 0))],
        out_shape=[jax.ShapeDtypeStruct((ATTN_HEADS, 2, t, t), F32),
                   jax.ShapeDtypeStruct((ATTN_HEADS, 2, t, LANES), F32)],
        name="bias_tables",
    )(rel_bias)


def _attn_kernel(q_ref, k_ref, v_ref, kp_ref, vp_ref, d_ref, p_ref, lam_ref, g_ref, o_ref,
                 s_ref, sp_ref, m_ref, l_ref, acc_ref, *, t, has_prefix, lam_init):
    i = pl.program_id(1)
    nc = t // LANES
    hm = [(h, m) for h in range(ATTN_HEADS) for m in range(2)]

    def cols(h, m):
        c0 = h * ATTN_DV + m * ATTN_DK
        return slice(c0, c0 + ATTN_DK)

    def vcols(h):
        return slice(h * ATTN_DV, (h + 1) * ATTN_DV)

    def lane_fold(x, op):
        out = x[:, :LANES]
        for c in range(1, x.shape[1] // LANES):
            out = op(out, x[:, c * LANES:(c + 1) * LANES])
        return out

    def logits(h, m, k_tile_ref, rows):
        return lax.dot_general(q_ref[:, cols(h, m)], k_tile_ref[rows, cols(h, m)], _NT,
                               preferred_element_type=F32)

    if has_prefix:
        first = jnp.minimum(i, 1)
        for h, m in hm:
            sp = logits(h, m, kp_ref, slice(None)) + p_ref[h, first]
            sp_ref[h, m] = sp
            m_ref[h, m] = sp
    else:
        m_ref[...] = jnp.full(m_ref.shape, -jnp.inf, F32)

    def score_tile(j, near):
        rows = pl.ds(pl.multiple_of(j * t, t), t)
        for h, m in hm:
            s = logits(h, m, k_ref, rows)
            if near:
                s = s + d_ref[h, i - j]
            s_ref[h, m, j] = s
            m_ref[h, m] = jnp.maximum(m_ref[h, m], lane_fold(s, jnp.maximum))

    def far_body(j, c):
        score_tile(j, False)
        return c

    def near_body(j, c):
        score_tile(j, True)
        return c

    lax.fori_loop(0, jnp.maximum(i - 1, 0), far_body, 0)
    lax.fori_loop(jnp.maximum(i - 1, 0), i + 1, near_body, 0)

    for h, m in hm:
        m_ref[h, m] = jnp.broadcast_to(jnp.max(m_ref[h, m], axis=-1, keepdims=True), (t, LANES))

    if has_prefix:
        for h, m in hm:
            p = jnp.exp(sp_ref[h, m] - m_ref[h, m])
            l_ref[h, m] = p
            acc_ref[h, m] = jnp.dot(p.astype(BF16), vp_ref[:, vcols(h)], preferred_element_type=F32)
    else:
        l_ref[...] = jnp.zeros_like(l_ref)
        acc_ref[...] = jnp.zeros_like(acc_ref)

    def value_body(j, c):
        rows = pl.ds(pl.multiple_of(j * t, t), t)
        for h, m in hm:
            p = jnp.exp(s_ref[h, m, j] - jnp.concatenate([m_ref[h, m]] * nc, axis=1))
            l_ref[h, m] += lane_fold(p, jnp.add)
            acc_ref[h, m] += jnp.dot(p.astype(BF16), v_ref[rows, vcols(h)], preferred_element_type=F32)
        return c

    lax.fori_loop(0, i + 1, value_body, 0)

    lam = lam_ref[0, 0]
    for h in range(ATTN_HEADS):
        o = [acc_ref[h, m] / jnp.sum(l_ref[h, m], axis=-1, keepdims=True) for m in range(2)]
        out = o[0] - lam * o[1]
        out = out * lax.rsqrt(jnp.mean(out * out, axis=-1, keepdims=True) + RMS_EPS)
        o_ref[:, vcols(h)] = (out * g_ref[...] * (1.0 - lam_init)).astype(BF16)


def _diff_attention(q, k, v, kp, vp, dtab, ptab, lam, subln_g, Bn, S, t, has_prefix, lam_init):
    nq = S // t
    W = ATTN_WIDTH
    hm_shape = (ATTN_HEADS, 2, t, LANES)
    return pl.pallas_call(
        functools.partial(_attn_kernel, t=t, has_prefix=has_prefix, lam_init=lam_init),
        grid=(Bn, nq),
        in_specs=[pl.BlockSpec((t, W), lambda b, i: (b * nq + i, 0)),
                  pl.BlockSpec((S, W), lambda b, i: (b, 0)),
                  pl.BlockSpec((S, W), lambda b, i: (b, 0)),
                  pl.BlockSpec((META_ROWS, W), lambda b, i: (0, 0)),
                  pl.BlockSpec((META_ROWS, W), lambda b, i: (0, 0)),
                  pl.BlockSpec((ATTN_HEADS, 2, t, t), lambda b, i: (0, 0, 0, 0)),
                  pl.BlockSpec((ATTN_HEADS, 2, t, LANES), lambda b, i: (0, 0, 0, 0)),
                  pl.BlockSpec(memory_space=pltpu.SMEM),
                  pl.BlockSpec((1, ATTN_DV), lambda b, i: (0, 0))],
        out_specs=pl.BlockSpec((t, W), lambda b, i: (b * nq + i, 0)),
        out_shape=jax.ShapeDtypeStruct((Bn * S, W), BF16),
        scratch_shapes=[pltpu.VMEM((ATTN_HEADS, 2, nq, t, t), F32),
                        pltpu.VMEM(hm_shape, F32),
                        pltpu.VMEM(hm_shape, F32),
                        pltpu.VMEM(hm_shape, F32),
                        pltpu.VMEM(hm_shape, F32)],
        compiler_params=pltpu.CompilerParams(
            dimension_semantics=("parallel", "arbitrary"), vmem_limit_bytes=VMEM_LIMIT),
        name="diff_attn",
    )(q, k, v, kp, vp, dtab, ptab, lam, subln_g.reshape(1, -1))


def _lam_kernel(q1_ref, k1_ref, q2_ref, k2_ref, o_ref, *, lam_init):
    a = jnp.sum(q1_ref[...] * k1_ref[...], axis=-1, keepdims=True)
    b = jnp.sum(q2_ref[...] * k2_ref[...], axis=-1, keepdims=True)
    o_ref[...] = jnp.broadcast_to(jnp.exp(a) - jnp.exp(b) + lam_init, o_ref.shape)


def _lambda(q1, k1, q2, k2, lam_init):
    r = lambda x: x.reshape(1, ATTN_DK)
    return pl.pallas_call(
        functools.partial(_lam_kernel, lam_init=lam_init),
        out_shape=jax.ShapeDtypeStruct((1, LANES), F32),
        name="lambda",
    )(r(q1), r(k1), r(q2), r(k2))


def _outproj_kernel(a_ref, c_ref, h_ref, w_ref, g_ref, b_ref, o_ref):
    mix = jnp.dot(a_ref[...], w_ref[0:ATTN_WIDTH, :], preferred_element_type=F32)
    mix = mix + jnp.dot(c_ref[...], w_ref[ATTN_WIDTH:, :], preferred_element_type=F32)
    o_ref[...] = _ln(DEEPNORM_ALPHA * h_ref[...] + mix, g_ref[...], b_ref[...])


def _outproj_ln(a, c, h, w_out, g, b, tm):
    T = h.shape[0]
    row = lambda i: (i, 0)
    full = lambda i: (0, 0)
    return pl.pallas_call(
        _outproj_kernel,
        grid=(T // tm,),
        in_specs=[pl.BlockSpec((tm, ATTN_WIDTH), row),
                  pl.BlockSpec((tm, CONV_WIDTH), row),
                  pl.BlockSpec((tm, D_MODEL), row),
                  pl.BlockSpec((D_MODEL, D_MODEL), full),
                  pl.BlockSpec((1, D_MODEL), full),
                  pl.BlockSpec((1, D_MODEL), full)],
        out_specs=pl.BlockSpec((tm, D_MODEL), row),
        out_shape=jax.ShapeDtypeStruct((T, D_MODEL), F32),
        compiler_params=pltpu.CompilerParams(
            dimension_semantics=("parallel",), vmem_limit_bytes=VMEM_LIMIT),
        name="outproj_ln",
    )(a, c, h, w_out, g.reshape(1, -1), b.reshape(1, -1))


NOT_SELECTED = 64.0


def _top16(s, with_rank):
    vals = []
    cur = s
    rank = jnp.full(s.shape, NOT_SELECTED, F32) if with_rank else None
    for i in range(PEER_TOPK):
        m = jnp.max(cur, axis=0, keepdims=True)
        vals.append(m)
        hit = cur == m
        if with_rank:
            rank = jnp.where(hit, float(i), rank)
        cur = jnp.where(hit, -jnp.inf, cur)
    return vals, rank


def _dup_bf16(x):
    b = pltpu.bitcast(x, jnp.uint32)
    hi = (b + jnp.uint32(0x7FFF) + ((b >> 16) & jnp.uint32(1))) & jnp.uint32(0xFFFF0000)
    return hi | (hi >> 16)


def _pair_threshold(c1, c2):
    c1s = jnp.concatenate(c1, axis=0)
    c2s = jnp.concatenate(c2, axis=0)
    lo = c2s[0:8, :]
    row = lax.broadcasted_iota(jnp.int32, lo.shape, 0)
    parts = [c1s + c2[0], c1s[0:8, :] + c2[1], c1s[0:8, :] + c2[2], c1s[0:8, :] + c2[3],
             c1[0] + c2s[8:16, :]]
    for i in range(3):
        parts.append(jnp.where(row >= 4, c1[i] + lo, -jnp.inf))
    cand = jnp.concatenate(parts, axis=0)
    cur = cand
    tau = None
    for _ in range(PEER_TOPK):
        tau = jnp.max(cur, axis=0, keepdims=True)
        cur = jnp.where(cur == tau, -jnp.inf, cur)
    top = c1[0] + c2[0]
    z = jnp.sum(jnp.where(cand >= tau, jnp.exp(cand - top), 0.0), axis=0, keepdims=True)
    return tau, z


def _select(s1, s2):
    c1, _ = _top16(s1, False)
    c2, rank2 = _top16(s2, True)
    tau, z = _pair_threshold(c1, c2)
    c2s = jnp.concatenate(c2, axis=0)
    count = jnp.zeros(s1.shape, F32)
    for i in range(PEER_TOPK):
        n = jnp.sum(jnp.where(c1[i] + c2s >= tau, 1.0, 0.0), axis=0, keepdims=True)
        count = jnp.where(s1 == c1[i], n, count)
    e1 = jnp.exp(s1 - c1[0]) / z
    e2 = jnp.exp(s2 - c2[0])
    return rank2.astype(BF16), e2.astype(BF16), _dup_bf16(count), _dup_bf16(e1)


def _peer_kernel(h_ref, wq_ref, sk_ref, u_ref, vt_ref, g_ref, b_ref, o_ref,
                 hb_ref, qp_ref, s_ref, r2_ref, e2_ref, n1_ref, e1_ref,
                 act_a, act_b, wact_a, wact_b, acc_ref, *, tT, E):
    j = pl.program_id(1)
    n_steps = pl.num_programs(1)
    nl = tT // LANES
    na = E // PEER_NKEYS
    n_blk = PEER_N // E
    RB = 16

    def phase_a(act, half):
        act[...] = lax.dot_general(u_ref[half * E:(half + 1) * E, :], hb_ref[...], _NT,
                                   preferred_element_type=F32)

    @pl.when(j == 0)
    def _():
        hb = h_ref[...].astype(BF16)
        hb_ref[...] = hb
        QC = 512
        for c in range(PEER_HEADS * PEER_DQ // QC):
            qp_ref[c * QC:(c + 1) * QC, :] = lax.dot_general(
                wq_ref[c * QC:(c + 1) * QC, :], hb, _NT, preferred_element_type=F32).astype(BF16)
        half = PEER_DQ // 2
        for hp in range(2 * PEER_HEADS):
            sT = jnp.dot(sk_ref[hp], qp_ref[hp * half:(hp + 1) * half, :], preferred_element_type=F32)
            for l in range(nl):
                s_ref[hp, l] = sT[:, l * LANES:(l + 1) * LANES]

        def select(idx, carry):
            h = idx // (nl // lg)
            l0 = (idx % (nl // lg)) * lg
            for dl in range(lg):
                l = l0 + dl
                r2_ref[h, l], e2_ref[h, l], n1_ref[h, l], e1_ref[h, l] = _select(s_ref[2 * h, l], s_ref[2 * h + 1, l])
            return carry

        lg = 2 if nl % 2 == 0 else 1
        lax.fori_loop(0, PEER_HEADS * nl // lg, select, 0)
        phase_a(act_a, 0)
        acc_ref[...] = jnp.zeros_like(acc_ref)

    def phase_v(act, wact, blk):
        blk = jnp.clip(blk, 0, n_blk - 1)

        def row_bf16(ref, h, l, a):
            return pltpu.bitcast(jnp.broadcast_to(ref[h, l, pl.ds(a, 1), :], (8, LANES)), BF16)

        nr = PEER_NKEYS // RB
        zero = jnp.zeros((), BF16)
        one = jnp.ones((), BF16)
        for l in range(nl):
            lanes = slice(l * LANES, (l + 1) * LANES)
            for aa in range(na):
                a = blk * na + aa
                w = [jnp.zeros((RB, LANES), BF16)] * nr
                for h in range(PEER_HEADS):
                    n1r = row_bf16(n1_ref, h, l, a)
                    e1r = row_bf16(e1_ref, h, l, a)
                    for r in range(nr):
                        r2c = r2_ref[h, l, r * RB:(r + 1) * RB, :]
                        e2c = e2_ref[h, l, r * RB:(r + 1) * RB, :]
                        sel = jnp.minimum(jnp.maximum(n1r - r2c, zero), one)
                        w[r] = w[r] + sel * (e1r * e2c)
                for r in range(nr):
                    rows = slice(aa * PEER_NKEYS + r * RB, aa * PEER_NKEYS + (r + 1) * RB)
                    x = act[rows, lanes]
                    gelu = 0.5 * x * (1.0 + lax.erf(x * SQRT_HALF))
                    wact[rows, lanes] = w[r] * gelu.astype(BF16)

    def phase_c(wact, half):
        acc_ref[...] += jnp.dot(vt_ref[:, half * E:(half + 1) * E], wact[...], preferred_element_type=F32)

    @pl.when(j >= 1)
    def _():
        phase_c(wact_a, 0)
        phase_v(act_b, wact_b, 2 * j - 1)
        phase_c(wact_b, 1)
        phase_a(act_a, 0)

    phase_v(act_a, wact_a, 2 * j)
    phase_a(act_b, 1)

    @pl.when(j == n_steps - 1)
    def _():
        y = DEEPNORM_ALPHA * h_ref[...] + acc_ref[...].T
        o_ref[...] = _ln(y, g_ref[...], b_ref[...])


def _peer_ln(h, wq_t, sk, u, v_t, g, b, tT, E):
    T = h.shape[0]
    nl = tT // LANES
    n_pairs = PEER_N // (2 * E)
    row = lambda i, j: (i, 0)
    full2 = lambda i, j: (0, 0)
    head_shape = (PEER_HEADS, nl, PEER_NKEYS, LANES)
    return pl.pallas_call(
        functools.partial(_peer_kernel, tT=tT, E=E),
        grid=(T // tT, n_pairs + 1),
        in_specs=[pl.BlockSpec((tT, D_MODEL), row),
                  pl.BlockSpec((PEER_HEADS * PEER_DQ, D_MODEL), full2, pipeline_mode=pl.Buffered(1)),
                  pl.BlockSpec((2 * PEER_HEADS, PEER_NKEYS, PEER_DQ // 2), lambda i, j: (0, 0, 0)),
                  pl.BlockSpec((2 * E, D_MODEL), lambda i, j: (jnp.minimum(j, n_pairs - 1), 0)),
                  pl.BlockSpec((D_MODEL, 2 * E), lambda i, j: (0, jnp.maximum(j - 1, 0))),
                  pl.BlockSpec((1, D_MODEL), full2),
                  pl.BlockSpec((1, D_MODEL), full2)],
        out_specs=pl.BlockSpec((tT, D_MODEL), row),
        out_shape=jax.ShapeDtypeStruct((T, D_MODEL), F32),
        scratch_shapes=[pltpu.VMEM((tT, D_MODEL), BF16),
                        pltpu.VMEM((PEER_HEADS * PEER_DQ, tT), BF16),
                        pltpu.VMEM((2 * PEER_HEADS, nl, PEER_NKEYS, LANES), F32),
                        pltpu.VMEM(head_shape, BF16),
                        pltpu.VMEM(head_shape, BF16),
                        pltpu.VMEM(head_shape, jnp.uint32),
                        pltpu.VMEM(head_shape, jnp.uint32),
                        pltpu.VMEM((E, tT), F32), pltpu.VMEM((E, tT), F32),
                        pltpu.VMEM((E, tT), BF16), pltpu.VMEM((E, tT), BF16),
                        pltpu.VMEM((D_MODEL, tT), F32)],
        compiler_params=pltpu.CompilerParams(
            dimension_semantics=("parallel", "arbitrary"), vmem_limit_bytes=VMEM_LIMIT),
        name="peer_ln",
    )(h, wq_t, sk, u, v_t, g.reshape(1, -1), b.reshape(1, -1))


def _tile(n, pref):
    return pref if n % pref == 0 else n


def _run_layer(l, h, Bn, S, prefix, params, tabs, lam, last_meta=False):
    (w_in, conv_w, subln_g, w_out, ln1_g, ln1_b, wq_t, sk, u, v_t, ln2_g, ln2_b) = params
    lam_init = 0.8 - 0.6 * math.exp(-0.3 * l)
    is_meta = prefix is None
    t = _tile(S, 256)
    ts = _tile(S, 512)
    halo = jnp.zeros((8, CONV_WIDTH), F32) if is_meta else prefix[2]
    outs = _proj_conv(h, w_in, conv_w, halo, Bn, S, ts, emit_u=is_meta)
    q, k, v, c = outs[:4]
    new_prefix = (k, v, outs[4][N_META - 8:N_META]) if is_meta else None
    if last_meta:
        return None, new_prefix
    kp, vp = (k, v) if is_meta else prefix[:2]
    a = _diff_attention(q, k, v, kp, vp, tabs[0], tabs[1], lam, subln_g, Bn, S, t,
                        has_prefix=not is_meta, lam_init=lam_init)
    T = Bn * S
    h1 = _outproj_ln(a, c, h, w_out, ln1_g, ln1_b, _tile(T, 512))
    h2 = _peer_ln(h1, wq_t, sk, u, v_t, ln2_g, ln2_b, _tile(T, 512), 512)
    return h2, new_prefix


def kernel(x, meta_tokens, ln_in_g, ln_in_b, rel_bias, w_in, conv_w, lambda_q1, lambda_k1, lambda_q2,
           lambda_k2, subln_g, w_out, ln1_g, ln1_b, peer_w_q, peer_sub_keys, peer_u, peer_v, ln2_g, ln2_b):
    Bn, S, _ = x.shape
    depth = w_in.shape[0]
    w_in_b = w_in.astype(BF16)
    w_out_b = w_out.astype(BF16)
    wq_t = jnp.swapaxes(peer_w_q, 1, 2).astype(BF16)
    sk = peer_sub_keys.reshape(depth, 2 * PEER_HEADS, PEER_NKEYS, PEER_DQ // 2).astype(BF16)
    u_b = peer_u.astype(BF16)
    v_t = jnp.swapaxes(peer_v, 1, 2).astype(BF16)

    t_main = _tile(S, 256)
    tabs_main = _bias_tables(rel_bias, t_main)
    tabs_meta = _bias_tables(rel_bias, META_ROWS)

    meta = jnp.zeros((META_ROWS, D_MODEL), F32).at[:N_META].set(meta_tokens.astype(F32))
    hm = _layer_norm_rows(meta, ln_in_g, ln_in_b, META_ROWS)
    h = _layer_norm_rows(x.reshape(Bn * S, D_MODEL), ln_in_g, ln_in_b, _tile(Bn * S, 512))

    for l in range(depth):
        lam_init = 0.8 - 0.6 * math.exp(-0.3 * l)
        lam = _lambda(lambda_q1[l], lambda_k1[l], lambda_q2[l], lambda_k2[l], lam_init)
        params = (w_in_b[l], conv_w[l], subln_g[l], w_out_b[l], ln1_g[l], ln1_b[l], wq_t[l], sk[l],
                  u_b[l], v_t[l], ln2_g[l], ln2_b[l])
        hm, prefix = _run_layer(l, hm, 1, META_ROWS, None, params, tabs_meta, lam,
                                last_meta=(l == depth - 1))
        h, _ = _run_layer(l, h, Bn, S, prefix, params, tabs_main, lam)
    return h.reshape(Bn, S, D_MODEL)
```

```python
import functools
import math

import jax
import jax.numpy as jnp
from jax import lax
from jax.experimental import pallas as pl
from jax.experimental.pallas import tpu as pltpu

F32 = jnp.float32
BF16 = jnp.bfloat16

D_MODEL = 1024
DEPTH = 4
N_META = 16
META_ROWS = 128
ATTN_HEADS = 4
ATTN_DK = 64
ATTN_DV = 2 * ATTN_DK
ATTN_WIDTH = ATTN_HEADS * ATTN_DV
CONV_WIDTH = D_MODEL - ATTN_WIDTH
CONV_K = 3
IN_COLS = 3 * ATTN_WIDTH + 3 * CONV_WIDTH
REL_BUCKETS = 32
REL_MAX_DIST = 128
PEER_HEADS = 8
PEER_NKEYS = 128
PEER_N = PEER_NKEYS * PEER_NKEYS
PEER_DQ = 256
PEER_TOPK = 16
DEEPNORM_ALPHA = (2 * DEPTH) ** 0.25
LN_EPS = 1e-5
RMS_EPS = 1e-5
NEG_BIG = -1e30
SQRT_HALF = math.sqrt(0.5)

PEER_BLOCK = 2048
LANES = 128
VMEM_LIMIT = 56 * 1024 * 1024

_NT = (((1,), (1,)), ((), ()))


def _ln(y, g, b):
    mu = jnp.mean(y, axis=-1, keepdims=True)
    yc = y - mu
    var = jnp.mean(yc * yc, axis=-1, keepdims=True)
    return yc * lax.rsqrt(var + LN_EPS) * g + b


def _ln_kernel(x_ref, g_ref, b_ref, o_ref):
    o_ref[...] = _ln(x_ref[...], g_ref[...], b_ref[...])


def _layer_norm_rows(x, g, b, tm):
    T = x.shape[0]
    return pl.pallas_call(
        _ln_kernel,
        grid=(T // tm,),
        in_specs=[pl.BlockSpec((tm, D_MODEL), lambda i: (i, 0)),
                  pl.BlockSpec((1, D_MODEL), lambda i: (0, 0)),
                  pl.BlockSpec((1, D_MODEL), lambda i: (0, 0))],
        out_specs=pl.BlockSpec((tm, D_MODEL), lambda i: (i, 0)),
        out_shape=jax.ShapeDtypeStruct((T, D_MODEL), F32),
        compiler_params=pltpu.CompilerParams(dimension_semantics=("parallel",)),
        name="ln_in",
    )(x, g.reshape(1, -1), b.reshape(1, -1))


def _proj_kernel(h_ref, w_ref, cw_ref, halo_ref, q_ref, k_ref, v_ref, c_ref, *rest, ts, emit_u):
    if emit_u:
        u_ref, ubuf = rest
    else:
        (ubuf,) = rest
    W = ATTN_WIDTH

    @pl.when(pl.program_id(1) == 0)
    def _():
        ubuf[0:8, :] = halo_ref[...]

    hb = h_ref[...].astype(BF16)

    def col(c):
        return jnp.dot(hb, w_ref[:, c * W:(c + 1) * W], preferred_element_type=F32)

    q_ref[...] = (col(0) * (ATTN_DK ** -0.5)).astype(BF16)
    k_ref[...] = col(1).astype(BF16)
    v_ref[...] = col(2).astype(BF16)
    u = col(4) * col(5)
    ubuf[8:8 + ts, :] = u
    if emit_u:
        u_ref[...] = u
    conv = (cw_ref[0:1, :] * ubuf[6:6 + ts, :] + cw_ref[1:2, :] * ubuf[7:7 + ts, :]
            + cw_ref[2:3, :] * u)
    c_ref[...] = (col(3) * conv).astype(BF16)
    ubuf[0:8, :] = ubuf[ts:ts + 8, :]


def _proj_conv(h, w_in, conv_w, halo, Bn, S, ts, emit_u):
    T = Bn * S
    nS = S // ts
    row = lambda b, s: (b * nS + s, 0)
    full = lambda b, s: (0, 0)
    out_shape = [jax.ShapeDtypeStruct((T, ATTN_WIDTH), BF16)] * 4
    out_specs = [pl.BlockSpec((ts, ATTN_WIDTH), row)] * 4
    if emit_u:
        out_shape = out_shape + [jax.ShapeDtypeStruct((T, CONV_WIDTH), F32)]
        out_specs = out_specs + [pl.BlockSpec((ts, CONV_WIDTH), row)]
    return pl.pallas_call(
        functools.partial(_proj_kernel, ts=ts, emit_u=emit_u),
        grid=(Bn, nS),
        in_specs=[pl.BlockSpec((ts, D_MODEL), row),
                  pl.BlockSpec((D_MODEL, IN_COLS), full),
                  pl.BlockSpec((CONV_K, CONV_WIDTH), full),
                  pl.BlockSpec((8, CONV_WIDTH), full)],
        out_specs=out_specs,
        out_shape=out_shape,
        scratch_shapes=[pltpu.VMEM((ts + 8, CONV_WIDTH), F32)],
        compiler_params=pltpu.CompilerParams(
            dimension_semantics=("parallel", "arbitrary"), vmem_limit_bytes=VMEM_LIMIT),
        name="proj_conv",
    )(h, w_in, conv_w, halo)


def _bias_kernel(rb_ref, d_ref, p_ref, *, t):
    h = pl.program_id(0)
    far = rb_ref[REL_BUCKETS - 1, h]

    def table(n):
        max_exact = REL_BUCKETS // 2
        nf = jnp.maximum(n, 1).astype(F32)
        large = max_exact + (jnp.log(nf / max_exact) / math.log(REL_MAX_DIST / max_exact)
                             * (REL_BUCKETS - max_exact)).astype(jnp.int32)
        large = jnp.minimum(large, REL_BUCKETS - 1)
        bucket = jnp.where(n < max_exact, n, large)
        out = jnp.zeros(n.shape, F32)
        for bkt in range(REL_BUCKETS):
            out = jnp.where(bucket == bkt, rb_ref[bkt, h], out)
        return out - far

    r = lax.broadcasted_iota(jnp.int32, (t, t), 0)
    c = lax.broadcasted_iota(jnp.int32, (t, t), 1)
    d_ref[0, 0] = jnp.where(c <= r, table(jnp.maximum(r - c, 0)), NEG_BIG)
    d_ref[0, 1] = table(t + r - c)
    rp = lax.broadcasted_iota(jnp.int32, (t, LANES), 0)
    cp = lax.broadcasted_iota(jnp.int32, (t, LANES), 1)
    p_ref[0, 0] = jnp.where(cp < N_META, table(N_META + rp - cp), NEG_BIG)
    p_ref[0, 1] = jnp.where(cp < N_META, 0.0, NEG_BIG)


def _bias_tables(rel_bias, t):
    return pl.pallas_call(
        functools.partial(_bias_kernel, t=t),
        grid=(ATTN_HEADS,),
        in_specs=[pl.BlockSpec(memory_space=pltpu.SMEM)],
        out_specs=[pl.BlockSpec((1, 2, t, t), lambda h: (h, 0, 0, 0)),
                   pl.BlockSpec((1, 2, t, LANES), lambda h: (h, 0, 0, 0))],
        out_shape=[jax.ShapeDtypeStruct((ATTN_HEADS, 2, t, t), F32),
                   jax.ShapeDtypeStruct((ATTN_HEADS, 2, t, LANES), F32)],
        name="bias_tables",
    )(rel_bias)


def _attn_kernel(q_ref, k_ref, v_ref, kp_ref, vp_ref, d_ref, p_ref, lam_ref, g_ref, o_ref,
                 s_ref, sp_ref, m_ref, l_ref, acc_ref, *, t, has_prefix, lam_init):
    i = pl.program_id(1)
    nc = t // LANES
    hm = [(h, m) for h in range(ATTN_HEADS) for m in range(2)]

    def cols(h, m):
        c0 = h * ATTN_DV + m * ATTN_DK
        return slice(c0, c0 + ATTN_DK)

    def vcols(h):
        return slice(h * ATTN_DV, (h + 1) * ATTN_DV)

    def lane_fold(x, op):
        out = x[:, :LANES]
        for c in range(1, x.shape[1] // LANES):
            out = op(out, x[:, c * LANES:(c + 1) * LANES])
        return out

    def logits(h, m, k_tile_ref, rows):
        return lax.dot_general(q_ref[:, cols(h, m)], k_tile_ref[rows, cols(h, m)], _NT,
                               preferred_element_type=F32)

    if has_prefix:
        first = jnp.minimum(i, 1)
        for h, m in hm:
            sp = logits(h, m, kp_ref, slice(None)) + p_ref[h, first]
            sp_ref[h, m] = sp
            m_ref[h, m] = sp
    else:
        m_ref[...] = jnp.full(m_ref.shape, -jnp.inf, F32)

    def score_tile(j, near):
        rows = pl.ds(pl.multiple_of(j * t, t), t)
        for h, m in hm:
            s = logits(h, m, k_ref, rows)
            if near:
                s = s + d_ref[h, i - j]
            s_ref[h, m, j] = s
            m_ref[h, m] = jnp.maximum(m_ref[h, m], lane_fold(s, jnp.maximum))

    def far_body(j, c):
        score_tile(j, False)
        return c

    def near_body(j, c):
        score_tile(j, True)
        return c

    lax.fori_loop(0, jnp.maximum(i - 1, 0), far_body, 0)
    lax.fori_loop(jnp.maximum(i - 1, 0), i + 1, near_body, 0)

    for h, m in hm:
        m_ref[h, m] = jnp.broadcast_to(jnp.max(m_ref[h, m], axis=-1, keepdims=True), (t, LANES))

    if has_prefix:
        for h, m in hm:
            p = jnp.exp(sp_ref[h, m] - m_ref[h, m])
            l_ref[h, m] = p
            acc_ref[h, m] = jnp.dot(p.astype(BF16), vp_ref[:, vcols(h)], preferred_element_type=F32)
    else:
        l_ref[...] = jnp.zeros_like(l_ref)
        acc_ref[...] = jnp.zeros_like(acc_ref)

    def value_body(j, c):
        rows = pl.ds(pl.multiple_of(j * t, t), t)
        for h, m in hm:
            p = jnp.exp(s_ref[h, m, j] - jnp.concatenate([m_ref[h, m]] * nc, axis=1))
            l_ref[h, m] += lane_fold(p, jnp.add)
            acc_ref[h, m] += jnp.dot(p.astype(BF16), v_ref[rows, vcols(h)], preferred_element_type=F32)
        return c

    lax.fori_loop(0, i + 1, value_body, 0)

    lam = lam_ref[0, 0]
    for h in range(ATTN_HEADS):
        o = [acc_ref[h, m] / jnp.sum(l_ref[h, m], axis=-1, keepdims=True) for m in range(2)]
        out = o[0] - lam * o[1]
        out = out * lax.rsqrt(jnp.mean(out * out, axis=-1, keepdims=True) + RMS_EPS)
        o_ref[:, vcols(h)] = (out * g_ref[...] * (1.0 - lam_init)).astype(BF16)


def _diff_attention(q, k, v, kp, vp, dtab, ptab, lam, subln_g, Bn, S, t, has_prefix, lam_init):
    nq = S // t
    W = ATTN_WIDTH
    hm_shape = (ATTN_HEADS, 2, t, LANES)
    return pl.pallas_call(
        functools.partial(_attn_kernel, t=t, has_prefix=has_prefix, lam_init=lam_init),
        grid=(Bn, nq),
        in_specs=[pl.BlockSpec((t, W), lambda b, i: (b * nq + i, 0)),
                  pl.BlockSpec((S, W), lambda b, i: (b, 0)),
                  pl.BlockSpec((S, W), lambda b, i: (b, 0)),
                  pl.BlockSpec((META_ROWS, W), lambda b, i: (0, 0)),
                  pl.BlockSpec((META_ROWS, W), lambda b, i: (0, 0)),
                  pl.BlockSpec((ATTN_HEADS, 2, t, t), lambda b, i: (0, 0, 0, 0)),
                  pl.BlockSpec((ATTN_HEADS, 2, t, LANES), lambda b, i: (0, 0, 0, 0)),
                  pl.BlockSpec(memory_space=pltpu.SMEM),
                  pl.BlockSpec((1, ATTN_DV), lambda b, i: (0, 0))],
        out_specs=pl.BlockSpec((t, W), lambda b, i: (b * nq + i, 0)),
        out_shape=jax.ShapeDtypeStruct((Bn * S, W), BF16),
        scratch_shapes=[pltpu.VMEM((ATTN_HEADS, 2, nq, t, t), F32),
                        pltpu.VMEM(hm_shape, F32),
                        pltpu.VMEM(hm_shape, F32),
                        pltpu.VMEM(hm_shape, F32),
                        pltpu.VMEM(hm_shape, F32)],
        compiler_params=pltpu.CompilerParams(
            dimension_semantics=("parallel", "arbitrary"), vmem_limit_bytes=VMEM_LIMIT),
        name="diff_attn",
    )(q, k, v, kp, vp, dtab, ptab, lam, subln_g.reshape(1, -1))


def _lam_kernel(q1_ref, k1_ref, q2_ref, k2_ref, o_ref, *, lam_init):
    a = jnp.sum(q1_ref[...] * k1_ref[...], axis=-1, keepdims=True)
    b = jnp.sum(q2_ref[...] * k2_ref[...], axis=-1, keepdims=True)
    o_ref[...] = jnp.broadcast_to(jnp.exp(a) - jnp.exp(b) + lam_init, o_ref.shape)


def _lambda(q1, k1, q2, k2, lam_init):
    r = lambda x: x.reshape(1, ATTN_DK)
    return pl.pallas_call(
        functools.partial(_lam_kernel, lam_init=lam_init),
        out_shape=jax.ShapeDtypeStruct((1, LANES), F32),
        name="lambda",
    )(r(q1), r(k1), r(q2), r(k2))


def _outproj_kernel(a_ref, c_ref, h_ref, w_ref, g_ref, b_ref, o_ref):
    mix = jnp.dot(a_ref[...], w_ref[0:ATTN_WIDTH, :], preferred_element_type=F32)
    mix = mix + jnp.dot(c_ref[...], w_ref[ATTN_WIDTH:, :], preferred_element_type=F32)
    o_ref[...] = _ln(DEEPNORM_ALPHA * h_ref[...] + mix, g_ref[...], b_ref[...])


def _outproj_ln(a, c, h, w_out, g, b, tm):
    T = h.shape[0]
    row = lambda i: (i, 0)
    full = lambda i: (0, 0)
    return pl.pallas_call(
        _outproj_kernel,
        grid=(T // tm,),
        in_specs=[pl.BlockSpec((tm, ATTN_WIDTH), row),
                  pl.BlockSpec((tm, CONV_WIDTH), row),
                  pl.BlockSpec((tm, D_MODEL), row),
                  pl.BlockSpec((D_MODEL, D_MODEL), full),
                  pl.BlockSpec((1, D_MODEL), full),
                  pl.BlockSpec((1, D_MODEL), full)],
        out_specs=pl.BlockSpec((tm, D_MODEL), row),
        out_shape=jax.ShapeDtypeStruct((T, D_MODEL), F32),
        compiler_params=pltpu.CompilerParams(
            dimension_semantics=("parallel",), vmem_limit_bytes=VMEM_LIMIT),
        name="outproj_ln",
    )(a, c, h, w_out, g.reshape(1, -1), b.reshape(1, -1))


NOT_SELECTED = 64.0


def _top16(s, with_rank):
    vals = []
    cur = s
    rank = jnp.full(s.shape, NOT_SELECTED, F32) if with_rank else None
    for i in range(PEER_TOPK):
        m = jnp.max(cur, axis=0, keepdims=True)
        vals.append(m)
        hit = cur == m
        if with_rank:
            rank = jnp.where(hit, float(i), rank)
        cur = jnp.where(hit, -jnp.inf, cur)
    return vals, rank


def _dup_bf16(x):
    b = pltpu.bitcast(x, jnp.uint32)
    hi = (b + jnp.uint32(0x7FFF) + ((b >> 16) & jnp.uint32(1))) & jnp.uint32(0xFFFF0000)
    return hi | (hi >> 16)


def _pair_threshold(c1, c2):
    c1s = jnp.concatenate(c1, axis=0)
    c2s = jnp.concatenate(c2, axis=0)
    lo = c2s[0:8, :]
    row = lax.broadcasted_iota(jnp.int32, lo.shape, 0)
    parts = [c1s + c2[0], c1s[0:8, :] + c2[1], c1s[0:8, :] + c2[2], c1s[0:8, :] + c2[3],
             c1[0] + c2s[8:16, :]]
    for i in range(3):
        parts.append(jnp.where(row >= 4, c1[i] + lo, -jnp.inf))
    cand = jnp.concatenate(parts, axis=0)
    cur = cand
    tau = None
    for _ in range(PEER_TOPK):
        tau = jnp.max(cur, axis=0, keepdims=True)
        cur = jnp.where(cur == tau, -jnp.inf, cur)
    top = c1[0] + c2[0]
    z = jnp.sum(jnp.where(cand >= tau, jnp.exp(cand - top), 0.0), axis=0, keepdims=True)
    return tau, z


def _select(s1, s2):
    c1, _ = _top16(s1, False)
    c2, rank2 = _top16(s2, True)
    tau, z = _pair_threshold(c1, c2)
    c2s = jnp.concatenate(c2, axis=0)
    count = jnp.zeros(s1.shape, F32)
    for i in range(PEER_TOPK):
        n = jnp.sum(jnp.where(c1[i] + c2s >= tau, 1.0, 0.0), axis=0, keepdims=True)
        count = jnp.where(s1 == c1[i], n, count)
    e1 = jnp.exp(s1 - c1[0]) / z
    e2 = jnp.exp(s2 - c2[0])
    return rank2.astype(BF16), e2.astype(BF16), _dup_bf16(count), _dup_bf16(e1)


def _peer_kernel(h_ref, wq_ref, sk_ref, u_ref, vt_ref, g_ref, b_ref, o_ref,
                 hb_ref, qp_ref, s_ref, r2_ref, e2_ref, n1_ref, e1_ref, act_ref, wact_ref, acc_ref, *, tT, EB):
    j = pl.program_id(1)
    nl = tT // LANES
    na = EB // PEER_NKEYS
    RB = 16
    AG = 4
    nr = PEER_NKEYS // RB

    @pl.when(j == 0)
    def _():
        hb = h_ref[...].astype(BF16)
        hb_ref[...] = hb
        QC = 512
        for c in range(PEER_HEADS * PEER_DQ // QC):
            qp_ref[c * QC:(c + 1) * QC, :] = lax.dot_general(
                wq_ref[c * QC:(c + 1) * QC, :], hb, _NT, preferred_element_type=F32).astype(BF16)
        half = PEER_DQ // 2
        for hp in range(2 * PEER_HEADS):
            sT = jnp.dot(sk_ref[hp], qp_ref[hp * half:(hp + 1) * half, :], preferred_element_type=F32)
            for l in range(nl):
                s_ref[hp, l] = sT[:, l * LANES:(l + 1) * LANES]

        def select(idx, carry):
            h = idx // (nl // lg)
            l0 = (idx % (nl // lg)) * lg
            for dl in range(lg):
                l = l0 + dl
                r2_ref[h, l], e2_ref[h, l], n1_ref[h, l], e1_ref[h, l] = _select(s_ref[2 * h, l], s_ref[2 * h + 1, l])
            return carry

        lg = 2 if nl % 2 == 0 else 1
        lax.fori_loop(0, PEER_HEADS * nl // lg, select, 0)
        acc_ref[...] = jnp.zeros_like(acc_ref)

    act_ref[...] = lax.dot_general(u_ref[...], hb_ref[...], _NT, preferred_element_type=F32)

    def row_bf16(ref, h, l, a):
        return pltpu.bitcast(jnp.broadcast_to(ref[h, l, pl.ds(a, 1), :], (8, LANES)), BF16)

    zero = jnp.zeros((), BF16)
    one = jnp.ones((), BF16)

    def gate_rows(t, carry):
        for l in range(nl):
            lanes = slice(l * LANES, (l + 1) * LANES)
            for aa in range(AG):
                al = t * AG + aa
                a = j * na + al
                w = [jnp.zeros((RB, LANES), BF16)] * nr
                for h in range(PEER_HEADS):
                    n1r = row_bf16(n1_ref, h, l, a)
                    e1r = row_bf16(e1_ref, h, l, a)
                    for r in range(nr):
                        r2c = r2_ref[h, l, r * RB:(r + 1) * RB, :]
                        e2c = e2_ref[h, l, r * RB:(r + 1) * RB, :]
                        sel = jnp.minimum(jnp.maximum(n1r - r2c, zero), one)
                        w[r] = w[r] + sel * (e1r * e2c)
                for r in range(nr):
                    rows = pl.ds(pl.multiple_of(al * PEER_NKEYS + r * RB, RB), RB)
                    x = act_ref[rows, lanes]
                    gelu = 0.5 * x * (1.0 + lax.erf(x * SQRT_HALF))
                    wact_ref[rows, lanes] = w[r] * gelu.astype(BF16)
        return carry

    lax.fori_loop(0, na // AG, gate_rows, 0)
    acc_ref[...] += jnp.dot(vt_ref[...], wact_ref[...], preferred_element_type=F32)

    @pl.when(j == pl.num_programs(1) - 1)
    def _():
        y = DEEPNORM_ALPHA * h_ref[...] + acc_ref[...].T
        o_ref[...] = _ln(y, g_ref[...], b_ref[...])


def _peer_ln(h, wq_t, sk, u, v_t, g, b, tT, EB):
    T = h.shape[0]
    nl = tT // LANES
    row = lambda i, j: (i, 0)
    full2 = lambda i, j: (0, 0)
    head_shape = (PEER_HEADS, nl, PEER_NKEYS, LANES)
    return pl.pallas_call(
        functools.partial(_peer_kernel, tT=tT, EB=EB),
        grid=(T // tT, PEER_N // EB),
        in_specs=[pl.BlockSpec((tT, D_MODEL), row),
                  pl.BlockSpec((PEER_HEADS * PEER_DQ, D_MODEL), full2, pipeline_mode=pl.Buffered(1)),
                  pl.BlockSpec((2 * PEER_HEADS, PEER_NKEYS, PEER_DQ // 2), lambda i, j: (0, 0, 0)),
                  pl.BlockSpec((EB, D_MODEL), lambda i, j: (j, 0)),
                  pl.BlockSpec((D_MODEL, EB), lambda i, j: (0, j)),
                  pl.BlockSpec((1, D_MODEL), full2),
                  pl.BlockSpec((1, D_MODEL), full2)],
        out_specs=pl.BlockSpec((tT, D_MODEL), row),
        out_shape=jax.ShapeDtypeStruct((T, D_MODEL), F32),
        scratch_shapes=[pltpu.VMEM((tT, D_MODEL), BF16),
                        pltpu.VMEM((PEER_HEADS * PEER_DQ, tT), BF16),
                        pltpu.VMEM((2 * PEER_HEADS, nl, PEER_NKEYS, LANES), F32),
                        pltpu.VMEM(head_shape, BF16),
                        pltpu.VMEM(head_shape, BF16),
                        pltpu.VMEM(head_shape, jnp.uint32),
                        pltpu.VMEM(head_shape, jnp.uint32),
                        pltpu.VMEM((EB, tT), F32),
                        pltpu.VMEM((EB, tT), BF16),
                        pltpu.VMEM((D_MODEL, tT), F32)],
        compiler_params=pltpu.CompilerParams(
            dimension_semantics=("parallel", "arbitrary"), vmem_limit_bytes=VMEM_LIMIT),
        name="peer_ln",
    )(h, wq_t, sk, u, v_t, g.reshape(1, -1), b.reshape(1, -1))


def _tile(n, pref):
    return pref if n % pref == 0 else n


def _run_layer(l, h, Bn, S, prefix, params, tabs, lam, last_meta=False):
    (w_in, conv_w, subln_g, w_out, ln1_g, ln1_b, wq_t, sk, u, v_t, ln2_g, ln2_b) = params
    lam_init = 0.8 - 0.6 * math.exp(-0.3 * l)
    is_meta = prefix is None
    t = _tile(S, 256)
    ts = _tile(S, 512)
    halo = jnp.zeros((8, CONV_WIDTH), F32) if is_meta else prefix[2]
    outs = _proj_conv(h, w_in, conv_w, halo, Bn, S, ts, emit_u=is_meta)
    q, k, v, c = outs[:4]
    new_prefix = (k, v, outs[4][N_META - 8:N_META]) if is_meta else None
    if last_meta:
        return None, new_prefix
    kp, vp = (k, v) if is_meta else prefix[:2]
    a = _diff_attention(q, k, v, kp, vp, tabs[0], tabs[1], lam, subln_g, Bn, S, t,
                        has_prefix=not is_meta, lam_init=lam_init)
    T = Bn * S
    h1 = _outproj_ln(a, c, h, w_out, ln1_g, ln1_b, _tile(T, 512))
    h2 = _peer_ln(h1, wq_t, sk, u, v_t, ln2_g, ln2_b, _tile(T, 512), PEER_BLOCK)
    return h2, new_prefix


def kernel(x, meta_tokens, ln_in_g, ln_in_b, rel_bias, w_in, conv_w, lambda_q1, lambda_k1, lambda_q2,
           lambda_k2, subln_g, w_out, ln1_g, ln1_b, peer_w_q, peer_sub_keys, peer_u, peer_v, ln2_g, ln2_b):
    Bn, S, _ = x.shape
    depth = w_in.shape[0]
    w_in_b = w_in.astype(BF16)
    w_out_b = w_out.astype(BF16)
    wq_t = jnp.swapaxes(peer_w_q, 1, 2).astype(BF16)
    sk = peer_sub_keys.reshape(depth, 2 * PEER_HEADS, PEER_NKEYS, PEER_DQ // 2).astype(BF16)
    u_b = peer_u.astype(BF16)
    v_t = jnp.swapaxes(peer_v, 1, 2).astype(BF16)

    t_main = _tile(S, 256)
    tabs_main = _bias_tables(rel_bias, t_main)
    tabs_meta = _bias_tables(rel_bias, META_ROWS)

    meta = jnp.zeros((META_ROWS, D_MODEL), F32).at[:N_META].set(meta_tokens.astype(F32))
    hm = _layer_norm_rows(meta, ln_in_g, ln_in_b, META_ROWS)
    h = _layer_norm_rows(x.reshape(Bn * S, D_MODEL), ln_in_g, ln_in_b, _tile(Bn * S, 512))

    for l in range(depth):
        lam_init = 0.8 - 0.6 * math.exp(-0.3 * l)
        lam = _lambda(lambda_q1[l], lambda_k1[l], lambda_q2[l], lambda_k2[l], lam_init)
        params = (w_in_b[l], conv_w[l], subln_g[l], w_out_b[l], ln1_g[l], ln1_b[l], wq_t[l], sk[l],
                  u_b[l], v_t[l], ln2_g[l], ln2_b[l])
        hm, prefix = _run_layer(l, hm, 1, META_ROWS, None, params, tabs_meta, lam,
                                last_meta=(l == depth - 1))
        h, _ = _run_layer(l, h, Bn, S, prefix, params, tabs_main, lam)
    return h.reshape(Bn, S, D_MODEL)
```

```python
import functools
import math

import jax
import jax.numpy as jnp
from jax import lax
from jax.experimental import pallas as pl
from jax.experimental.pallas import tpu as pltpu

F32 = jnp.float32
BF16 = jnp.bfloat16

D_MODEL = 1024
DEPTH = 4
N_META = 16
META_ROWS = 128
ATTN_HEADS = 4
ATTN_DK = 64
ATTN_DV = 2 * ATTN_DK
ATTN_WIDTH = ATTN_HEADS * ATTN_DV
CONV_WIDTH = D_MODEL - ATTN_WIDTH
CONV_K = 3
IN_COLS = 3 * ATTN_WIDTH + 3 * CONV_WIDTH
REL_BUCKETS = 32
REL_MAX_DIST = 128
PEER_HEADS = 8
PEER_NKEYS = 128
PEER_N = PEER_NKEYS * PEER_NKEYS
PEER_DQ = 256
PEER_TOPK = 16
DEEPNORM_ALPHA = (2 * DEPTH) ** 0.25
LN_EPS = 1e-5
RMS_EPS = 1e-5
NEG_BIG = -1e30
SQRT_HALF = math.sqrt(0.5)

PEER_BLOCK = 2048
LANES = 128
VMEM_LIMIT = 60000 * 1024

_NT = (((1,), (1,)), ((), ()))


def _ln(y, g, b):
    mu = jnp.mean(y, axis=-1, keepdims=True)
    yc = y - mu
    var = jnp.mean(yc * yc, axis=-1, keepdims=True)
    return yc * lax.rsqrt(var + LN_EPS) * g + b


def _ln_kernel(x_ref, g_ref, b_ref, o_ref):
    o_ref[...] = _ln(x_ref[...], g_ref[...], b_ref[...])


def _layer_norm_rows(x, g, b, tm):
    T = x.shape[0]
    return pl.pallas_call(
        _ln_kernel,
        grid=(T // tm,),
        in_specs=[pl.BlockSpec((tm, D_MODEL), lambda i: (i, 0)),
                  pl.BlockSpec((1, D_MODEL), lambda i: (0, 0)),
                  pl.BlockSpec((1, D_MODEL), lambda i: (0, 0))],
        out_specs=pl.BlockSpec((tm, D_MODEL), lambda i: (i, 0)),
        out_shape=jax.ShapeDtypeStruct((T, D_MODEL), F32),
        compiler_params=pltpu.CompilerParams(dimension_semantics=("parallel",)),
        name="ln_in",
    )(x, g.reshape(1, -1), b.reshape(1, -1))


def _proj_kernel(h_ref, w_ref, cw_ref, halo_ref, q_ref, k_ref, v_ref, c_ref, *rest, ts, emit_u):
    if emit_u:
        u_ref, ubuf = rest
    else:
        (ubuf,) = rest
    W = ATTN_WIDTH

    @pl.when(pl.program_id(1) == 0)
    def _():
        ubuf[0:8, :] = halo_ref[...]

    hb = h_ref[...].astype(BF16)

    def col(c):
        return jnp.dot(hb, w_ref[:, c * W:(c + 1) * W], preferred_element_type=F32)

    q_ref[...] = (col(0) * (ATTN_DK ** -0.5)).astype(BF16)
    k_ref[...] = col(1).astype(BF16)
    v_ref[...] = col(2).astype(BF16)
    u = col(4) * col(5)
    ubuf[8:8 + ts, :] = u
    if emit_u:
        u_ref[...] = u
    conv = (cw_ref[0:1, :] * ubuf[6:6 + ts, :] + cw_ref[1:2, :] * ubuf[7:7 + ts, :]
            + cw_ref[2:3, :] * u)
    c_ref[...] = (col(3) * conv).astype(BF16)
    ubuf[0:8, :] = ubuf[ts:ts + 8, :]


def _proj_conv(h, w_in, conv_w, halo, Bn, S, ts, emit_u):
    T = Bn * S
    nS = S // ts
    row = lambda b, s: (b * nS + s, 0)
    full = lambda b, s: (0, 0)
    out_shape = [jax.ShapeDtypeStruct((T, ATTN_WIDTH), BF16)] * 4
    out_specs = [pl.BlockSpec((ts, ATTN_WIDTH), row)] * 4
    if emit_u:
        out_shape = out_shape + [jax.ShapeDtypeStruct((T, CONV_WIDTH), F32)]
        out_specs = out_specs + [pl.BlockSpec((ts, CONV_WIDTH), row)]
    return pl.pallas_call(
        functools.partial(_proj_kernel, ts=ts, emit_u=emit_u),
        grid=(Bn, nS),
        in_specs=[pl.BlockSpec((ts, D_MODEL), row),
                  pl.BlockSpec((D_MODEL, IN_COLS), full),
                  pl.BlockSpec((CONV_K, CONV_WIDTH), full),
                  pl.BlockSpec((8, CONV_WIDTH), full)],
        out_specs=out_specs,
        out_shape=out_shape,
        scratch_shapes=[pltpu.VMEM((ts + 8, CONV_WIDTH), F32)],
        compiler_params=pltpu.CompilerParams(
            dimension_semantics=("parallel", "arbitrary"), vmem_limit_bytes=VMEM_LIMIT),
        name="proj_conv",
    )(h, w_in, conv_w, halo)


def _bias_kernel(rb_ref, d_ref, p_ref, *, t):
    h = pl.program_id(0)
    far = rb_ref[REL_BUCKETS - 1, h]

    def table(n):
        max_exact = REL_BUCKETS // 2
        nf = jnp.maximum(n, 1).astype(F32)
        large = max_exact + (jnp.log(nf / max_exact) / math.log(REL_MAX_DIST / max_exact)
                             * (REL_BUCKETS - max_exact)).astype(jnp.int32)
        large = jnp.minimum(large, REL_BUCKETS - 1)
        bucket = jnp.where(n < max_exact, n, large)
        out = jnp.zeros(n.shape, F32)
        for bkt in range(REL_BUCKETS):
            out = jnp.where(bucket == bkt, rb_ref[bkt, h], out)
        return out - far

    r = lax.broadcasted_iota(jnp.int32, (t, t), 0)
    c = lax.broadcasted_iota(jnp.int32, (t, t), 1)
    d_ref[0, 0] = jnp.where(c <= r, table(jnp.maximum(r - c, 0)), NEG_BIG)
    d_ref[0, 1] = table(t + r - c)
    rp = lax.broadcasted_iota(jnp.int32, (t, LANES), 0)
    cp = lax.broadcasted_iota(jnp.int32, (t, LANES), 1)
    p_ref[0, 0] = jnp.where(cp < N_META, table(N_META + rp - cp), NEG_BIG)
    p_ref[0, 1] = jnp.where(cp < N_META, 0.0, NEG_BIG)


def _bias_tables(rel_bias, t):
    return pl.pallas_call(
        functools.partial(_bias_kernel, t=t),
        grid=(ATTN_HEADS,),
        in_specs=[pl.BlockSpec(memory_space=pltpu.SMEM)],
        out_specs=[pl.BlockSpec((1, 2, t, t), lambda h: (h, 0, 0, 0)),
                   pl.BlockSpec((1, 2, t, LANES), lambda h: (h, 0, 0, 0))],
        out_shape=[jax.ShapeDtypeStruct((ATTN_HEADS, 2, t, t), F32),
                   jax.ShapeDtypeStruct((ATTN_HEADS, 2, t, LANES), F32)],
        name="bias_tables",
    )(rel_bias)


def _attn_kernel(q_ref, k_ref, v_ref, kp_ref, vp_ref, d_ref, p_ref, lam_ref, g_ref, o_ref,
                 s_ref, sp_ref, m_ref, l_ref, acc_ref, *, t, has_prefix, lam_init):
    i = pl.program_id(1)
    nc = t // LANES
    hm = [(h, m) for h in range(ATTN_HEADS) for m in range(2)]

    def cols(h, m):
        c0 = h * ATTN_DV + m * ATTN_DK
        return slice(c0, c0 + ATTN_DK)

    def vcols(h):
        return slice(h * ATTN_DV, (h + 1) * ATTN_DV)

    def lane_fold(x, op):
        out = x[:, :LANES]
        for c in range(1, x.shape[1] // LANES):
            out = op(out, x[:, c * LANES:(c + 1) * LANES])
        return out

    def logits(h, m, k_tile_ref, rows):
        return lax.dot_general(q_ref[:, cols(h, m)], k_tile_ref[rows, cols(h, m)], _NT,
                               preferred_element_type=F32)

    if has_prefix:
        first = jnp.minimum(i, 1)
        for h, m in hm:
            sp = logits(h, m, kp_ref, slice(None)) + p_ref[h, first]
            sp_ref[h, m] = sp
            m_ref[h, m] = sp
    else:
        m_ref[...] = jnp.full(m_ref.shape, -jnp.inf, F32)

    def score_tile(j, near):
        rows = pl.ds(pl.multiple_of(j * t, t), t)
        for h, m in hm:
            s = logits(h, m, k_ref, rows)
            if near:
                s = s + d_ref[h, i - j]
            s_ref[h, m, j] = s
            m_ref[h, m] = jnp.maximum(m_ref[h, m], lane_fold(s, jnp.maximum))

    def far_body(j, c):
        score_tile(j, False)
        return c

    def near_body(j, c):
        score_tile(j, True)
        return c

    lax.fori_loop(0, jnp.maximum(i - 1, 0), far_body, 0)
    lax.fori_loop(jnp.maximum(i - 1, 0), i + 1, near_body, 0)

    for h, m in hm:
        m_ref[h, m] = jnp.broadcast_to(jnp.max(m_ref[h, m], axis=-1, keepdims=True), (t, LANES))

    if has_prefix:
        for h, m in hm:
            p = jnp.exp(sp_ref[h, m] - m_ref[h, m])
            l_ref[h, m] = p
            acc_ref[h, m] = jnp.dot(p.astype(BF16), vp_ref[:, vcols(h)], preferred_element_type=F32)
    else:
        l_ref[...] = jnp.zeros_like(l_ref)
        acc_ref[...] = jnp.zeros_like(acc_ref)

    def value_body(j, c):
        rows = pl.ds(pl.multiple_of(j * t, t), t)
        for h, m in hm:
            p = jnp.exp(s_ref[h, m, j] - jnp.concatenate([m_ref[h, m]] * nc, axis=1))
            l_ref[h, m] += lane_fold(p, jnp.add)
            acc_ref[h, m] += jnp.dot(p.astype(BF16), v_ref[rows, vcols(h)], preferred_element_type=F32)
        return c

    lax.fori_loop(0, i + 1, value_body, 0)

    lam = lam_ref[0, 0]
    for h in range(ATTN_HEADS):
        o = [acc_ref[h, m] / jnp.sum(l_ref[h, m], axis=-1, keepdims=True) for m in range(2)]
        out = o[0] - lam * o[1]
        out = out * lax.rsqrt(jnp.mean(out * out, axis=-1, keepdims=True) + RMS_EPS)
        o_ref[:, vcols(h)] = (out * g_ref[...] * (1.0 - lam_init)).astype(BF16)


def _diff_attention(q, k, v, kp, vp, dtab, ptab, lam, subln_g, Bn, S, t, has_prefix, lam_init):
    nq = S // t
    W = ATTN_WIDTH
    hm_shape = (ATTN_HEADS, 2, t, LANES)
    return pl.pallas_call(
        functools.partial(_attn_kernel, t=t, has_prefix=has_prefix, lam_init=lam_init),
        grid=(Bn, nq),
        in_specs=[pl.BlockSpec((t, W), lambda b, i: (b * nq + i, 0)),
                  pl.BlockSpec((S, W), lambda b, i: (b, 0)),
                  pl.BlockSpec((S, W), lambda b, i: (b, 0)),
                  pl.BlockSpec((META_ROWS, W), lambda b, i: (0, 0)),
                  pl.BlockSpec((META_ROWS, W), lambda b, i: (0, 0)),
                  pl.BlockSpec((ATTN_HEADS, 2, t, t), lambda b, i: (0, 0, 0, 0)),
                  pl.BlockSpec((ATTN_HEADS, 2, t, LANES), lambda b, i: (0, 0, 0, 0)),
                  pl.BlockSpec(memory_space=pltpu.SMEM),
                  pl.BlockSpec((1, ATTN_DV), lambda b, i: (0, 0))],
        out_specs=pl.BlockSpec((t, W), lambda b, i: (b * nq + i, 0)),
        out_shape=jax.ShapeDtypeStruct((Bn * S, W), BF16),
        scratch_shapes=[pltpu.VMEM((ATTN_HEADS, 2, nq, t, t), F32),
                        pltpu.VMEM(hm_shape, F32),
                        pltpu.VMEM(hm_shape, F32),
                        pltpu.VMEM(hm_shape, F32),
                        pltpu.VMEM(hm_shape, F32)],
        compiler_params=pltpu.CompilerParams(
            dimension_semantics=("parallel", "arbitrary"), vmem_limit_bytes=VMEM_LIMIT),
        name="diff_attn",
    )(q, k, v, kp, vp, dtab, ptab, lam, subln_g.reshape(1, -1))


def _lam_kernel(q1_ref, k1_ref, q2_ref, k2_ref, o_ref, *, lam_init):
    a = jnp.sum(q1_ref[...] * k1_ref[...], axis=-1, keepdims=True)
    b = jnp.sum(q2_ref[...] * k2_ref[...], axis=-1, keepdims=True)
    o_ref[...] = jnp.broadcast_to(jnp.exp(a) - jnp.exp(b) + lam_init, o_ref.shape)


def _lambda(q1, k1, q2, k2, lam_init):
    r = lambda x: x.reshape(1, ATTN_DK)
    return pl.pallas_call(
        functools.partial(_lam_kernel, lam_init=lam_init),
        out_shape=jax.ShapeDtypeStruct((1, LANES), F32),
        name="lambda",
    )(r(q1), r(k1), r(q2), r(k2))


def _outproj_kernel(a_ref, c_ref, h_ref, w_ref, g_ref, b_ref, o_ref):
    mix = jnp.dot(a_ref[...], w_ref[0:ATTN_WIDTH, :], preferred_element_type=F32)
    mix = mix + jnp.dot(c_ref[...], w_ref[ATTN_WIDTH:, :], preferred_element_type=F32)
    o_ref[...] = _ln(DEEPNORM_ALPHA * h_ref[...] + mix, g_ref[...], b_ref[...])


def _outproj_ln(a, c, h, w_out, g, b, tm):
    T = h.shape[0]
    row = lambda i: (i, 0)
    full = lambda i: (0, 0)
    return pl.pallas_call(
        _outproj_kernel,
        grid=(T // tm,),
        in_specs=[pl.BlockSpec((tm, ATTN_WIDTH), row),
                  pl.BlockSpec((tm, CONV_WIDTH), row),
                  pl.BlockSpec((tm, D_MODEL), row),
                  pl.BlockSpec((D_MODEL, D_MODEL), full),
                  pl.BlockSpec((1, D_MODEL), full),
                  pl.BlockSpec((1, D_MODEL), full)],
        out_specs=pl.BlockSpec((tm, D_MODEL), row),
        out_shape=jax.ShapeDtypeStruct((T, D_MODEL), F32),
        compiler_params=pltpu.CompilerParams(
            dimension_semantics=("parallel",), vmem_limit_bytes=VMEM_LIMIT),
        name="outproj_ln",
    )(a, c, h, w_out, g.reshape(1, -1), b.reshape(1, -1))


NOT_SELECTED = 64.0


def _top16(s, with_rank, exact):
    vals = []
    cur = s
    rank = jnp.full(s.shape, NOT_SELECTED, F32) if with_rank else None
    rows = lax.broadcasted_iota(jnp.int32, s.shape, 0).astype(F32) if exact else None
    for i in range(PEER_TOPK):
        m = jnp.max(cur, axis=0, keepdims=True)
        vals.append(m)
        hit = cur == m
        if exact:
            hit = rows == jnp.min(jnp.where(hit, rows, float(PEER_NKEYS)), axis=0, keepdims=True)
        if with_rank:
            rank = jnp.where(hit, float(i), rank)
        cur = jnp.where(hit, -jnp.inf, cur)
    removed = jnp.sum(jnp.where(cur == -jnp.inf, 1.0, 0.0), axis=0, keepdims=True)
    return vals, rank, removed


def _dup_bf16(x):
    b = pltpu.bitcast(x, jnp.uint32)
    hi = (b + jnp.uint32(0x7FFF) + ((b >> 16) & jnp.uint32(1))) & jnp.uint32(0xFFFF0000)
    return hi | (hi >> 16)


def _pair_candidates(c1, c2):
    c1s = jnp.concatenate(c1, axis=0)
    c2s = jnp.concatenate(c2, axis=0)
    lo = c2s[0:8, :]
    row = lax.broadcasted_iota(jnp.int32, lo.shape, 0)
    parts = [c1s + c2[0], c1s[0:8, :] + c2[1], c1s[0:8, :] + c2[2], c1s[0:8, :] + c2[3],
             c1[0] + c2s[8:16, :]]
    for i in range(3):
        parts.append(jnp.where(row >= 4, c1[i] + lo, -jnp.inf))
    return jnp.concatenate(parts, axis=0), c2s


N_MASKED_CANDIDATES = 12


def _gate_tables(s1, s2, c1, c2, rank2, count, z):
    e1 = jnp.exp(s1 - c1[0]) * (0.5 / z)
    e2 = jnp.exp(s2 - c2[0])
    return rank2.astype(BF16), e2.astype(BF16), _dup_bf16(count), _dup_bf16(e1)


def _route_distinct(s1, s2, c1, c2, rank2):
    cand, c2s = _pair_candidates(c1, c2)
    cur = cand
    tau = None
    for _ in range(PEER_TOPK):
        tau = jnp.max(cur, axis=0, keepdims=True)
        cur = jnp.where(cur == tau, -jnp.inf, cur)
    removed = jnp.sum(jnp.where(cur == -jnp.inf, 1.0, 0.0), axis=0, keepdims=True) - float(N_MASKED_CANDIDATES)
    z = jnp.sum(jnp.where(cand >= tau, jnp.exp(cand - (c1[0] + c2[0])), 0.0), axis=0, keepdims=True)
    count = jnp.zeros(s1.shape, F32)
    for i in range(PEER_TOPK):
        n = jnp.sum(jnp.where(c1[i] + c2s >= tau, 1.0, 0.0), axis=0, keepdims=True)
        count = jnp.where(s1 == c1[i], n, count)
    return _gate_tables(s1, s2, c1, c2, rank2, count, z), removed


def _route_ties(s1, s2):
    c1, rank1, _ = _top16(s1, True, True)
    c2, rank2, _ = _top16(s2, True, True)
    cand, c2s = _pair_candidates(c1, c2)
    top = c1[0] + c2[0]
    cur = cand
    tau = top
    got = jnp.zeros_like(top)
    above = got
    for _ in range(PEER_TOPK):
        m = jnp.max(cur, axis=0, keepdims=True)
        hit = cur == m
        filling = got < float(PEER_TOPK)
        tau = jnp.where(filling, m, tau)
        above = jnp.where(filling, got, above)
        got = got + jnp.sum(jnp.where(hit, 1.0, 0.0), axis=0, keepdims=True)
        cur = jnp.where(hit, -jnp.inf, cur)
    at_tau = float(PEER_TOPK) - above
    z = jnp.sum(jnp.where(cand > tau, jnp.exp(cand - top), 0.0), axis=0, keepdims=True)
    z = z + at_tau * jnp.exp(tau - top)
    count = jnp.zeros(s1.shape, F32)
    used = jnp.zeros_like(tau)
    for i in range(PEER_TOPK):
        sums = c1[i] + c2s
        n_gt = jnp.sum(jnp.where(sums > tau, 1.0, 0.0), axis=0, keepdims=True)
        n_eq = jnp.sum(jnp.where(sums == tau, 1.0, 0.0), axis=0, keepdims=True)
        take = jnp.minimum(n_eq, jnp.maximum(at_tau - used, 0.0))
        used = used + take
        count = jnp.where(rank1 == float(i), n_gt + take, count)
    return _gate_tables(s1, s2, c1, c2, rank2, count, z)


def _select(s1, s2):
    c1, _, removed1 = _top16(s1, False, False)
    c2, rank2, removed2 = _top16(s2, True, False)
    tables, removed3 = _route_distinct(s1, s2, c1, c2, rank2)
    k = float(PEER_TOPK)
    tied = jnp.max(jnp.abs(removed1 - k) + jnp.abs(removed2 - k) + jnp.abs(removed3 - k)) > 0.0
    return tables, tied


def _peer_kernel(h_ref, wq_ref, sk_ref, u_ref, vt_ref, g_ref, b_ref, o_ref,
                 hb_ref, qp_ref, s_ref, r2_ref, e2_ref, n1_ref, e1_ref, tied_ref, act_ref, wact_ref, acc_ref,
                 *, tT, EB):
    j = pl.program_id(1)
    nl = tT // LANES
    na = EB // PEER_NKEYS
    RB = 16
    AG = 4
    nr = PEER_NKEYS // RB

    @pl.when(j == 0)
    def _():
        hb = h_ref[...].astype(BF16)
        hb_ref[...] = hb
        QC = 512
        for c in range(PEER_HEADS * PEER_DQ // QC):
            qp_ref[c * QC:(c + 1) * QC, :] = lax.dot_general(
                wq_ref[c * QC:(c + 1) * QC, :], hb, _NT, preferred_element_type=F32).astype(BF16)
        half = PEER_DQ // 2
        for hp in range(2 * PEER_HEADS):
            sT = jnp.dot(sk_ref[hp], qp_ref[hp * half:(hp + 1) * half, :], preferred_element_type=F32)
            for l in range(nl):
                s_ref[hp, l] = sT[:, l * LANES:(l + 1) * LANES]

        def select(idx, carry):
            h = idx // (nl // lg)
            l0 = (idx % (nl // lg)) * lg
            for dl in range(lg):
                l = l0 + dl
                tables, tied = _select(s_ref[2 * h, l], s_ref[2 * h + 1, l])
                r2_ref[h, l], e2_ref[h, l], n1_ref[h, l], e1_ref[h, l] = tables
                tied_ref[h * nl + l] = tied.astype(jnp.int32)
            return carry

        lg = 2 if nl % 2 == 0 else 1
        lax.fori_loop(0, PEER_HEADS * nl // lg, select, 0)

        def redo_ties(idx, carry):
            @pl.when(tied_ref[idx] != 0)
            def _():
                h = idx // nl
                l = idx % nl
                r2_ref[h, l], e2_ref[h, l], n1_ref[h, l], e1_ref[h, l] = _route_ties(s_ref[2 * h, l], s_ref[2 * h + 1, l])
            return carry

        lax.fori_loop(0, PEER_HEADS * nl, redo_ties, 0)
        acc_ref[...] = jnp.zeros_like(acc_ref)

    act_ref[...] = lax.dot_general(u_ref[...], hb_ref[...], _NT, preferred_element_type=F32)

    def row_bf16(ref, h, l, a):
        return pltpu.bitcast(jnp.broadcast_to(ref[h, l, pl.ds(a, 1), :], (8, LANES)), BF16)

    zero = jnp.zeros((), BF16)
    one = jnp.ones((), BF16)

    def gate_rows(t, carry):
        for l in range(nl):
            lanes = slice(l * LANES, (l + 1) * LANES)
            for aa in range(AG):
                al = t * AG + aa
                a = j * na + al
                w = [jnp.zeros((RB, LANES), BF16)] * nr
                for h in range(PEER_HEADS):
                    n1r = row_bf16(n1_ref, h, l, a)
                    e1r = row_bf16(e1_ref, h, l, a)
                    for r in range(nr):
                        r2c = r2_ref[h, l, r * RB:(r + 1) * RB, :]
                        e2c = e2_ref[h, l, r * RB:(r + 1) * RB, :]
                        sel = jnp.minimum(jnp.maximum(n1r - r2c, zero), one)
                        w[r] = w[r] + sel * (e1r * e2c)
                for r in range(nr):
                    rows = pl.ds(pl.multiple_of(al * PEER_NKEYS + r * RB, RB), RB)
                    x = act_ref[rows, lanes]
                    gelu2 = x * (1.0 + lax.erf(x * SQRT_HALF))
                    wact_ref[rows, lanes] = w[r] * gelu2.astype(BF16)
        return carry

    lax.fori_loop(0, na // AG, gate_rows, 0)
    acc_ref[...] += jnp.dot(vt_ref[...], wact_ref[...], preferred_element_type=F32)

    @pl.when(j == pl.num_programs(1) - 1)
    def _():
        y = DEEPNORM_ALPHA * h_ref[...] + acc_ref[...].T
        o_ref[...] = _ln(y, g_ref[...], b_ref[...])


def _peer_ln(h, wq_t, sk, u, v_t, g, b, tT, EB):
    T = h.shape[0]
    nl = tT // LANES
    row = lambda i, j: (i, 0)
    full2 = lambda i, j: (0, 0)
    head_shape = (PEER_HEADS, nl, PEER_NKEYS, LANES)
    return pl.pallas_call(
        functools.partial(_peer_kernel, tT=tT, EB=EB),
        grid=(T // tT, PEER_N // EB),
        in_specs=[pl.BlockSpec((tT, D_MODEL), row),
                  pl.BlockSpec((PEER_HEADS * PEER_DQ, D_MODEL), full2, pipeline_mode=pl.Buffered(1)),
                  pl.BlockSpec((2 * PEER_HEADS, PEER_NKEYS, PEER_DQ // 2), lambda i, j: (0, 0, 0)),
                  pl.BlockSpec((EB, D_MODEL), lambda i, j: (j, 0)),
                  pl.BlockSpec((D_MODEL, EB), lambda i, j: (0, j)),
                  pl.BlockSpec((1, D_MODEL), full2),
                  pl.BlockSpec((1, D_MODEL), full2)],
        out_specs=pl.BlockSpec((tT, D_MODEL), row),
        out_shape=jax.ShapeDtypeStruct((T, D_MODEL), F32),
        scratch_shapes=[pltpu.VMEM((tT, D_MODEL), BF16),
                        pltpu.VMEM((PEER_HEADS * PEER_DQ, tT), BF16),
                        pltpu.VMEM((2 * PEER_HEADS, nl, PEER_NKEYS, LANES), F32),
                        pltpu.VMEM(head_shape, BF16),
                        pltpu.VMEM(head_shape, BF16),
                        pltpu.VMEM(head_shape, jnp.uint32),
                        pltpu.VMEM(head_shape, jnp.uint32),
                        pltpu.SMEM((PEER_HEADS * nl,), jnp.int32),
                        pltpu.VMEM((EB, tT), F32),
                        pltpu.VMEM((EB, tT), BF16),
                        pltpu.VMEM((D_MODEL, tT), F32)],
        compiler_params=pltpu.CompilerParams(
            dimension_semantics=("parallel", "arbitrary"), vmem_limit_bytes=VMEM_LIMIT),
        name="peer_ln",
    )(h, wq_t, sk, u, v_t, g.reshape(1, -1), b.reshape(1, -1))


def _tile(n, pref):
    return pref if n % pref == 0 else n


def _run_layer(l, h, Bn, S, prefix, params, tabs, lam, last_meta=False):
    (w_in, conv_w, subln_g, w_out, ln1_g, ln1_b, wq_t, sk, u, v_t, ln2_g, ln2_b) = params
    lam_init = 0.8 - 0.6 * math.exp(-0.3 * l)
    is_meta = prefix is None
    t = _tile(S, 256)
    ts = _tile(S, 512)
    halo = jnp.zeros((8, CONV_WIDTH), F32) if is_meta else prefix[2]
    outs = _proj_conv(h, w_in, conv_w, halo, Bn, S, ts, emit_u=is_meta)
    q, k, v, c = outs[:4]
    new_prefix = (k, v, outs[4][N_META - 8:N_META]) if is_meta else None
    if last_meta:
        return None, new_prefix
    kp, vp = (k, v) if is_meta else prefix[:2]
    a = _diff_attention(q, k, v, kp, vp, tabs[0], tabs[1], lam, subln_g, Bn, S, t,
                        has_prefix=not is_meta, lam_init=lam_init)
    T = Bn * S
    h1 = _outproj_ln(a, c, h, w_out, ln1_g, ln1_b, _tile(T, 512))
    h2 = _peer_ln(h1, wq_t, sk, u, v_t, ln2_g, ln2_b, _tile(T, 512), PEER_BLOCK)
    return h2, new_prefix


def kernel(x, meta_tokens, ln_in_g, ln_in_b, rel_bias, w_in, conv_w, lambda_q1, lambda_k1, lambda_q2,
           lambda_k2, subln_g, w_out, ln1_g, ln1_b, peer_w_q, peer_sub_keys, peer_u, peer_v, ln2_g, ln2_b):
    Bn, S, _ = x.shape
    depth = w_in.shape[0]
    w_in_b = w_in.astype(BF16)
    w_out_b = w_out.astype(BF16)
    wq_t = jnp.swapaxes(peer_w_q, 1, 2).astype(BF16)
    sk = peer_sub_keys.reshape(depth, 2 * PEER_HEADS, PEER_NKEYS, PEER_DQ // 2).astype(BF16)
    u_b = peer_u.astype(BF16)
    v_t = jnp.swapaxes(peer_v, 1, 2).astype(BF16)

    t_main = _tile(S, 256)
    tabs_main = _bias_tables(rel_bias, t_main)
    tabs_meta = _bias_tables(rel_bias, META_ROWS)

    meta = jnp.zeros((META_ROWS, D_MODEL), F32).at[:N_META].set(meta_tokens.astype(F32))
    hm = _layer_norm_rows(meta, ln_in_g, ln_in_b, META_ROWS)
    h = _layer_norm_rows(x.reshape(Bn * S, D_MODEL), ln_in_g, ln_in_b, _tile(Bn * S, 512))

    for l in range(depth):
        lam_init = 0.8 - 0.6 * math.exp(-0.3 * l)
        lam = _lambda(lambda_q1[l], lambda_k1[l], lambda_q2[l], lambda_k2[l], lam_init)
        params = (w_in_b[l], conv_w[l], subln_g[l], w_out_b[l], ln1_g[l], ln1_b[l], wq_t[l], sk[l],
                  u_b[l], v_t[l], ln2_g[l], ln2_b[l])
        hm, prefix = _run_layer(l, hm, 1, META_ROWS, None, params, tabs_meta, lam,
                                last_meta=(l == depth - 1))
        h, _ = _run_layer(l, h, Bn, S, prefix, params, tabs_main, lam)
    return h.reshape(Bn, S, D_MODEL)
```

```python
import functools
import math

import jax
import jax.numpy as jnp
from jax import lax
from jax.experimental import pallas as pl
from jax.experimental.pallas import tpu as pltpu

F32 = jnp.float32
BF16 = jnp.bfloat16

D_MODEL = 1024
DEPTH = 4
N_META = 16
META_ROWS = 128
ATTN_HEADS = 4
ATTN_DK = 64
ATTN_DV = 2 * ATTN_DK
ATTN_WIDTH = ATTN_HEADS * ATTN_DV
CONV_WIDTH = D_MODEL - ATTN_WIDTH
CONV_K = 3
IN_COLS = 3 * ATTN_WIDTH + 3 * CONV_WIDTH
REL_BUCKETS = 32
REL_MAX_DIST = 128
PEER_HEADS = 8
PEER_NKEYS = 128
PEER_N = PEER_NKEYS * PEER_NKEYS
PEER_DQ = 256
PEER_TOPK = 16
DEEPNORM_ALPHA = (2 * DEPTH) ** 0.25
LN_EPS = 1e-5
RMS_EPS = 1e-5
NEG_BIG = -1e30
SQRT_HALF = math.sqrt(0.5)

PEER_BLOCK = 2048
LANES = 128
VMEM_LIMIT = 60000 * 1024

_NT = (((1,), (1,)), ((), ()))


def _ln(y, g, b):
    mu = jnp.mean(y, axis=-1, keepdims=True)
    yc = y - mu
    var = jnp.mean(yc * yc, axis=-1, keepdims=True)
    return yc * lax.rsqrt(var + LN_EPS) * g + b


def _ln_kernel(x_ref, g_ref, b_ref, o_ref):
    o_ref[...] = _ln(x_ref[...], g_ref[...], b_ref[...])


def _layer_norm_rows(x, g, b, tm):
    T = x.shape[0]
    return pl.pallas_call(
        _ln_kernel,
        grid=(T // tm,),
        in_specs=[pl.BlockSpec((tm, D_MODEL), lambda i: (i, 0)),
                  pl.BlockSpec((1, D_MODEL), lambda i: (0, 0)),
                  pl.BlockSpec((1, D_MODEL), lambda i: (0, 0))],
        out_specs=pl.BlockSpec((tm, D_MODEL), lambda i: (i, 0)),
        out_shape=jax.ShapeDtypeStruct((T, D_MODEL), F32),
        compiler_params=pltpu.CompilerParams(dimension_semantics=("parallel",)),
        name="ln_in",
    )(x, g.reshape(1, -1), b.reshape(1, -1))


def _proj_kernel(h_ref, w_ref, cw_ref, halo_ref, q_ref, k_ref, v_ref, c_ref, *rest, ts, emit_u):
    if emit_u:
        u_ref, ubuf = rest
    else:
        (ubuf,) = rest
    W = ATTN_WIDTH

    @pl.when(pl.program_id(1) == 0)
    def _():
        ubuf[0:8, :] = halo_ref[...]

    hb = h_ref[...].astype(BF16)

    def col(c):
        return jnp.dot(hb, w_ref[:, c * W:(c + 1) * W], preferred_element_type=F32)

    q_ref[...] = (col(0) * (ATTN_DK ** -0.5)).astype(BF16)
    k_ref[...] = col(1).astype(BF16)
    v_ref[...] = col(2).astype(BF16)
    u = col(4) * col(5)
    ubuf[8:8 + ts, :] = u
    if emit_u:
        u_ref[...] = u
    conv = (cw_ref[0:1, :] * ubuf[6:6 + ts, :] + cw_ref[1:2, :] * ubuf[7:7 + ts, :]
            + cw_ref[2:3, :] * u)
    c_ref[...] = (col(3) * conv).astype(BF16)
    ubuf[0:8, :] = ubuf[ts:ts + 8, :]


def _proj_conv(h, w_in, conv_w, halo, Bn, S, ts, emit_u):
    T = Bn * S
    nS = S // ts
    row = lambda b, s: (b * nS + s, 0)
    full = lambda b, s: (0, 0)
    out_shape = [jax.ShapeDtypeStruct((T, ATTN_WIDTH), BF16)] * 4
    out_specs = [pl.BlockSpec((ts, ATTN_WIDTH), row)] * 4
    if emit_u:
        out_shape = out_shape + [jax.ShapeDtypeStruct((T, CONV_WIDTH), F32)]
        out_specs = out_specs + [pl.BlockSpec((ts, CONV_WIDTH), row)]
    return pl.pallas_call(
        functools.partial(_proj_kernel, ts=ts, emit_u=emit_u),
        grid=(Bn, nS),
        in_specs=[pl.BlockSpec((ts, D_MODEL), row),
                  pl.BlockSpec((D_MODEL, IN_COLS), full),
                  pl.BlockSpec((CONV_K, CONV_WIDTH), full),
                  pl.BlockSpec((8, CONV_WIDTH), full)],
        out_specs=out_specs,
        out_shape=out_shape,
        scratch_shapes=[pltpu.VMEM((ts + 8, CONV_WIDTH), F32)],
        compiler_params=pltpu.CompilerParams(
            dimension_semantics=("parallel", "arbitrary"), vmem_limit_bytes=VMEM_LIMIT),
        name="proj_conv",
    )(h, w_in, conv_w, halo)


def _bias_kernel(rb_ref, d_ref, p_ref, *, t):
    h = pl.program_id(0)
    far = rb_ref[REL_BUCKETS - 1, h]

    def table(n):
        max_exact = REL_BUCKETS // 2
        nf = jnp.maximum(n, 1).astype(F32)
        large = max_exact + (jnp.log(nf / max_exact) / math.log(REL_MAX_DIST / max_exact)
                             * (REL_BUCKETS - max_exact)).astype(jnp.int32)
        large = jnp.minimum(large, REL_BUCKETS - 1)
        bucket = jnp.where(n < max_exact, n, large)
        out = jnp.zeros(n.shape, F32)
        for bkt in range(REL_BUCKETS):
            out = jnp.where(bucket == bkt, rb_ref[bkt, h], out)
        return out - far

    r = lax.broadcasted_iota(jnp.int32, (t, t), 0)
    c = lax.broadcasted_iota(jnp.int32, (t, t), 1)
    d_ref[0, 0] = jnp.where(c <= r, table(jnp.maximum(r - c, 0)), NEG_BIG)
    d_ref[0, 1] = table(t + r - c)
    rp = lax.broadcasted_iota(jnp.int32, (t, LANES), 0)
    cp = lax.broadcasted_iota(jnp.int32, (t, LANES), 1)
    p_ref[0, 0] = jnp.where(cp < N_META, table(N_META + rp - cp), NEG_BIG)
    p_ref[0, 1] = jnp.where(cp < N_META, 0.0, NEG_BIG)


def _bias_tables(rel_bias, t):
    return pl.pallas_call(
        functools.partial(_bias_kernel, t=t),
        grid=(ATTN_HEADS,),
        in_specs=[pl.BlockSpec(memory_space=pltpu.SMEM)],
        out_specs=[pl.BlockSpec((1, 2, t, t), lambda h: (h, 0, 0, 0)),
                   pl.BlockSpec((1, 2, t, LANES), lambda h: (h, 0, 0, 0))],
        out_shape=[jax.ShapeDtypeStruct((ATTN_HEADS, 2, t, t), F32),
                   jax.ShapeDtypeStruct((ATTN_HEADS, 2, t, LANES), F32)],
        name="bias_tables",
    )(rel_bias)


def _attn_kernel(q_ref, k_ref, v_ref, kp_ref, vp_ref, d_ref, p_ref, lam_ref, g_ref, o_ref,
                 s_ref, sp_ref, m_ref, l_ref, acc_ref, *, t, has_prefix, lam_init):
    i = pl.program_id(1)
    nc = t // LANES
    hm = [(h, m) for h in range(ATTN_HEADS) for m in range(2)]

    def cols(h, m):
        c0 = h * ATTN_DV + m * ATTN_DK
        return slice(c0, c0 + ATTN_DK)

    def vcols(h):
        return slice(h * ATTN_DV, (h + 1) * ATTN_DV)

    def lane_fold(x, op):
        out = x[:, :LANES]
        for c in range(1, x.shape[1] // LANES):
            out = op(out, x[:, c * LANES:(c + 1) * LANES])
        return out

    def logits(h, m, k_tile_ref, rows):
        return lax.dot_general(q_ref[:, cols(h, m)], k_tile_ref[rows, cols(h, m)], _NT,
                               preferred_element_type=F32)

    if has_prefix:
        first = jnp.minimum(i, 1)
        for h, m in hm:
            sp = logits(h, m, kp_ref, slice(None)) + p_ref[h, first]
            sp_ref[h, m] = sp
            m_ref[h, m] = sp
    else:
        m_ref[...] = jnp.full(m_ref.shape, -jnp.inf, F32)

    def score_tile(j, near):
        rows = pl.ds(pl.multiple_of(j * t, t), t)
        for h, m in hm:
            s = logits(h, m, k_ref, rows)
            if near:
                s = s + d_ref[h, i - j]
            s_ref[h, m, j] = s
            m_ref[h, m] = jnp.maximum(m_ref[h, m], lane_fold(s, jnp.maximum))

    def far_body(j, c):
        score_tile(j, False)
        return c

    def near_body(j, c):
        score_tile(j, True)
        return c

    lax.fori_loop(0, jnp.maximum(i - 1, 0), far_body, 0)
    lax.fori_loop(jnp.maximum(i - 1, 0), i + 1, near_body, 0)

    for h, m in hm:
        m_ref[h, m] = jnp.broadcast_to(jnp.max(m_ref[h, m], axis=-1, keepdims=True), (t, LANES))

    if has_prefix:
        for h, m in hm:
            p = jnp.exp(sp_ref[h, m] - m_ref[h, m])
            l_ref[h, m] = p
            acc_ref[h, m] = jnp.dot(p.astype(BF16), vp_ref[:, vcols(h)], preferred_element_type=F32)
    else:
        l_ref[...] = jnp.zeros_like(l_ref)
        acc_ref[...] = jnp.zeros_like(acc_ref)

    def value_body(j, c):
        rows = pl.ds(pl.multiple_of(j * t, t), t)
        for h, m in hm:
            p = jnp.exp(s_ref[h, m, j] - jnp.concatenate([m_ref[h, m]] * nc, axis=1))
            l_ref[h, m] += lane_fold(p, jnp.add)
            acc_ref[h, m] += jnp.dot(p.astype(BF16), v_ref[rows, vcols(h)], preferred_element_type=F32)
        return c

    lax.fori_loop(0, i + 1, value_body, 0)

    lam = lam_ref[0, 0]
    for h in range(ATTN_HEADS):
        o = [acc_ref[h, m] / jnp.sum(l_ref[h, m], axis=-1, keepdims=True) for m in range(2)]
        out = o[0] - lam * o[1]
        out = out * lax.rsqrt(jnp.mean(out * out, axis=-1, keepdims=True) + RMS_EPS)
        o_ref[:, vcols(h)] = (out * g_ref[...] * (1.0 - lam_init)).astype(BF16)


def _diff_attention(q, k, v, kp, vp, dtab, ptab, lam, subln_g, Bn, S, t, has_prefix, lam_init):
    nq = S // t
    W = ATTN_WIDTH
    hm_shape = (ATTN_HEADS, 2, t, LANES)
    return pl.pallas_call(
        functools.partial(_attn_kernel, t=t, has_prefix=has_prefix, lam_init=lam_init),
        grid=(Bn, nq),
        in_specs=[pl.BlockSpec((t, W), lambda b, i: (b * nq + i, 0)),
                  pl.BlockSpec((S, W), lambda b, i: (b, 0)),
                  pl.BlockSpec((S, W), lambda b, i: (b, 0)),
                  pl.BlockSpec((META_ROWS, W), lambda b, i: (0, 0)),
                  pl.BlockSpec((META_ROWS, W), lambda b, i: (0, 0)),
                  pl.BlockSpec((ATTN_HEADS, 2, t, t), lambda b, i: (0, 0, 0, 0)),
                  pl.BlockSpec((ATTN_HEADS, 2, t, LANES), lambda b, i: (0, 0, 0, 0)),
                  pl.BlockSpec(memory_space=pltpu.SMEM),
                  pl.BlockSpec((1, ATTN_DV), lambda b, i: (0, 0))],
        out_specs=pl.BlockSpec((t, W), lambda b, i: (b * nq + i, 0)),
        out_shape=jax.ShapeDtypeStruct((Bn * S, W), BF16),
        scratch_shapes=[pltpu.VMEM((ATTN_HEADS, 2, nq, t, t), F32),
                        pltpu.VMEM(hm_shape, F32),
                        pltpu.VMEM(hm_shape, F32),
                        pltpu.VMEM(hm_shape, F32),
                        pltpu.VMEM(hm_shape, F32)],
        compiler_params=pltpu.CompilerParams(
            dimension_semantics=("parallel", "arbitrary"), vmem_limit_bytes=VMEM_LIMIT),
        name="diff_attn",
    )(q, k, v, kp, vp, dtab, ptab, lam, subln_g.reshape(1, -1))


def _lam_kernel(q1_ref, k1_ref, q2_ref, k2_ref, o_ref, *, lam_init):
    a = jnp.sum(q1_ref[...] * k1_ref[...], axis=-1, keepdims=True)
    b = jnp.sum(q2_ref[...] * k2_ref[...], axis=-1, keepdims=True)
    o_ref[...] = jnp.broadcast_to(jnp.exp(a) - jnp.exp(b) + lam_init, o_ref.shape)


def _lambda(q1, k1, q2, k2, lam_init):
    r = lambda x: x.reshape(1, ATTN_DK)
    return pl.pallas_call(
        functools.partial(_lam_kernel, lam_init=lam_init),
        out_shape=jax.ShapeDtypeStruct((1, LANES), F32),
        name="lambda",
    )(r(q1), r(k1), r(q2), r(k2))


def _outproj_kernel(a_ref, c_ref, h_ref, w_ref, g_ref, b_ref, o_ref):
    mix = jnp.dot(a_ref[...], w_ref[0:ATTN_WIDTH, :], preferred_element_type=F32)
    mix = mix + jnp.dot(c_ref[...], w_ref[ATTN_WIDTH:, :], preferred_element_type=F32)
    o_ref[...] = _ln(DEEPNORM_ALPHA * h_ref[...] + mix, g_ref[...], b_ref[...])


def _outproj_ln(a, c, h, w_out, g, b, tm):
    T = h.shape[0]
    row = lambda i: (i, 0)
    full = lambda i: (0, 0)
    return pl.pallas_call(
        _outproj_kernel,
        grid=(T // tm,),
        in_specs=[pl.BlockSpec((tm, ATTN_WIDTH), row),
                  pl.BlockSpec((tm, CONV_WIDTH), row),
                  pl.BlockSpec((tm, D_MODEL), row),
                  pl.BlockSpec((D_MODEL, D_MODEL), full),
                  pl.BlockSpec((1, D_MODEL), full),
                  pl.BlockSpec((1, D_MODEL), full)],
        out_specs=pl.BlockSpec((tm, D_MODEL), row),
        out_shape=jax.ShapeDtypeStruct((T, D_MODEL), F32),
        compiler_params=pltpu.CompilerParams(
            dimension_semantics=("parallel",), vmem_limit_bytes=VMEM_LIMIT),
        name="outproj_ln",
    )(a, c, h, w_out, g.reshape(1, -1), b.reshape(1, -1))


NOT_SELECTED = 64.0


def _top16(s, with_rank, exact):
    vals = []
    cur = s
    rank = jnp.full(s.shape, NOT_SELECTED, F32) if with_rank else None
    rows = lax.broadcasted_iota(jnp.int32, s.shape, 0).astype(F32) if exact else None
    for i in range(PEER_TOPK):
        m = jnp.max(cur, axis=0, keepdims=True)
        vals.append(m)
        hit = cur == m
        if exact:
            hit = rows == jnp.min(jnp.where(hit, rows, float(PEER_NKEYS)), axis=0, keepdims=True)
        if with_rank:
            rank = jnp.where(hit, float(i), rank)
        cur = jnp.where(hit, -jnp.inf, cur)
    removed = jnp.sum(jnp.where(cur == -jnp.inf, 1.0, 0.0), axis=0, keepdims=True)
    return vals, rank, removed


def _dup_bf16(x):
    b = pltpu.bitcast(x, jnp.uint32)
    hi = (b + jnp.uint32(0x7FFF) + ((b >> 16) & jnp.uint32(1))) & jnp.uint32(0xFFFF0000)
    return hi | (hi >> 16)


def _pair_candidates(c1, c2):
    c1s = jnp.concatenate(c1, axis=0)
    c2s = jnp.concatenate(c2, axis=0)
    lo = c2s[0:8, :]
    row = lax.broadcasted_iota(jnp.int32, lo.shape, 0)
    parts = [c1s + c2[0], c1s[0:8, :] + c2[1], c1s[0:8, :] + c2[2], c1s[0:8, :] + c2[3],
             c1[0] + c2s[8:16, :]]
    for i in range(3):
        parts.append(jnp.where(row >= 4, c1[i] + lo, -jnp.inf))
    return jnp.concatenate(parts, axis=0), c2s


N_MASKED_CANDIDATES = 12


def _gate_tables(s1, s2, c1, c2, rank2, count, z):
    e1 = jnp.exp(s1 - c1[0]) * (0.5 / z)
    e2 = jnp.exp(s2 - c2[0])
    return rank2.astype(BF16), e2.astype(BF16), _dup_bf16(count), _dup_bf16(e1)


def _route_distinct(s1, s2, c1, c2, rank2):
    cand, c2s = _pair_candidates(c1, c2)
    cur = cand
    tau = None
    for _ in range(PEER_TOPK):
        tau = jnp.max(cur, axis=0, keepdims=True)
        cur = jnp.where(cur == tau, -jnp.inf, cur)
    removed = jnp.sum(jnp.where(cur == -jnp.inf, 1.0, 0.0), axis=0, keepdims=True) - float(N_MASKED_CANDIDATES)
    z = jnp.sum(jnp.where(cand >= tau, jnp.exp(cand - (c1[0] + c2[0])), 0.0), axis=0, keepdims=True)
    count = jnp.zeros(s1.shape, F32)
    for i in range(PEER_TOPK):
        n = jnp.sum(jnp.where(c1[i] + c2s >= tau, 1.0, 0.0), axis=0, keepdims=True)
        count = jnp.where(s1 == c1[i], n, count)
    return _gate_tables(s1, s2, c1, c2, rank2, count, z), removed


def _route_ties(s1, s2):
    c1, rank1, _ = _top16(s1, True, True)
    c2, rank2, _ = _top16(s2, True, True)
    cand, c2s = _pair_candidates(c1, c2)
    top = c1[0] + c2[0]
    cur = cand
    tau = top
    got = jnp.zeros_like(top)
    above = got
    for _ in range(PEER_TOPK):
        m = jnp.max(cur, axis=0, keepdims=True)
        hit = cur == m
        filling = got < float(PEER_TOPK)
        tau = jnp.where(filling, m, tau)
        above = jnp.where(filling, got, above)
        got = got + jnp.sum(jnp.where(hit, 1.0, 0.0), axis=0, keepdims=True)
        cur = jnp.where(hit, -jnp.inf, cur)
    at_tau = float(PEER_TOPK) - above
    z = jnp.sum(jnp.where(cand > tau, jnp.exp(cand - top), 0.0), axis=0, keepdims=True)
    z = z + at_tau * jnp.exp(tau - top)
    count = jnp.zeros(s1.shape, F32)
    used = jnp.zeros_like(tau)
    for i in range(PEER_TOPK):
        sums = c1[i] + c2s
        n_gt = jnp.sum(jnp.where(sums > tau, 1.0, 0.0), axis=0, keepdims=True)
        n_eq = jnp.sum(jnp.where(sums == tau, 1.0, 0.0), axis=0, keepdims=True)
        take = jnp.minimum(n_eq, jnp.maximum(at_tau - used, 0.0))
        used = used + take
        count = jnp.where(rank1 == float(i), n_gt + take, count)
    return _gate_tables(s1, s2, c1, c2, rank2, count, z)


def _select(s1, s2):
    c1, _, removed1 = _top16(s1, False, False)
    c2, rank2, removed2 = _top16(s2, True, False)
    tables, removed3 = _route_distinct(s1, s2, c1, c2, rank2)
    k = float(PEER_TOPK)
    tied = jnp.max(jnp.abs(removed1 - k) + jnp.abs(removed2 - k) + jnp.abs(removed3 - k)) > 0.0
    return tables, tied


def _peer_kernel(h_ref, wq_ref, sk_ref, u_ref, vt_ref, g_ref, b_ref, o_ref,
                 hb_ref, qp_ref, s_ref, r2_ref, e2_ref, n1_ref, e1_ref, tied_ref, act_ref, wact_ref, acc_ref,
                 *, tT, EB):
    j = pl.program_id(1)
    nl = tT // LANES
    na = EB // PEER_NKEYS
    RB = 16
    AG = 4
    nr = PEER_NKEYS // RB

    @pl.when(j == 0)
    def _():
        hb = h_ref[...].astype(BF16)
        hb_ref[...] = hb
        QC = 512
        for c in range(PEER_HEADS * PEER_DQ // QC):
            qp_ref[c * QC:(c + 1) * QC, :] = lax.dot_general(
                wq_ref[c * QC:(c + 1) * QC, :], hb, _NT, preferred_element_type=F32).astype(BF16)
        half = PEER_DQ // 2
        for hp in range(2 * PEER_HEADS):
            sT = jnp.dot(sk_ref[hp], qp_ref[hp * half:(hp + 1) * half, :], preferred_element_type=F32)
            for l in range(nl):
                s_ref[hp, l] = sT[:, l * LANES:(l + 1) * LANES]

        def select(idx, carry):
            h = idx // (nl // lg)
            l0 = (idx % (nl // lg)) * lg
            for dl in range(lg):
                l = l0 + dl
                tables, tied = _select(s_ref[2 * h, l], s_ref[2 * h + 1, l])
                r2_ref[h, l], e2_ref[h, l], n1_ref[h, l], e1_ref[h, l] = tables
                tied_ref[h * nl + l] = tied.astype(jnp.int32)
            return carry

        lg = 2 if nl % 2 == 0 else 1
        lax.fori_loop(0, PEER_HEADS * nl // lg, select, 0)

        def redo_ties(idx, carry):
            @pl.when(tied_ref[idx] != 0)
            def _():
                h = idx // nl
                l = idx % nl
                r2_ref[h, l], e2_ref[h, l], n1_ref[h, l], e1_ref[h, l] = _route_ties(s_ref[2 * h, l], s_ref[2 * h + 1, l])
            return carry

        lax.fori_loop(0, PEER_HEADS * nl, redo_ties, 0)
        acc_ref[...] = jnp.zeros_like(acc_ref)

    act_ref[...] = lax.dot_general(u_ref[...], hb_ref[...], _NT, preferred_element_type=F32)

    def row_bf16(ref, h, l, a):
        return pltpu.bitcast(jnp.broadcast_to(ref[h, l, pl.ds(a, 1), :], (8, LANES)), BF16)

    zero = jnp.zeros((), BF16)

    def gate_rows(t, carry):
        for l in range(nl):
            lanes = slice(l * LANES, (l + 1) * LANES)
            for aa in range(AG):
                al = t * AG + aa
                a = j * na + al
                w = [jnp.zeros((RB, LANES), BF16)] * nr
                for h in range(PEER_HEADS):
                    n1r = row_bf16(n1_ref, h, l, a)
                    e1r = row_bf16(e1_ref, h, l, a)
                    for r in range(nr):
                        r2c = r2_ref[h, l, r * RB:(r + 1) * RB, :]
                        e2c = e2_ref[h, l, r * RB:(r + 1) * RB, :]
                        w[r] = w[r] + jnp.minimum(e1r * e2c, jnp.maximum(n1r - r2c, zero))
                for r in range(nr):
                    rows = pl.ds(pl.multiple_of(al * PEER_NKEYS + r * RB, RB), RB)
                    x = act_ref[rows, lanes]
                    gelu2 = x * (1.0 + lax.erf(x * SQRT_HALF))
                    wact_ref[rows, lanes] = w[r] * gelu2.astype(BF16)
        return carry

    lax.fori_loop(0, na // AG, gate_rows, 0)
    acc_ref[...] += jnp.dot(vt_ref[...], wact_ref[...], preferred_element_type=F32)

    @pl.when(j == pl.num_programs(1) - 1)
    def _():
        y = DEEPNORM_ALPHA * h_ref[...] + acc_ref[...].T
        o_ref[...] = _ln(y, g_ref[...], b_ref[...])


def _peer_ln(h, wq_t, sk, u, v_t, g, b, tT, EB):
    T = h.shape[0]
    nl = tT // LANES
    row = lambda i, j: (i, 0)
    full2 = lambda i, j: (0, 0)
    head_shape = (PEER_HEADS, nl, PEER_NKEYS, LANES)
    return pl.pallas_call(
        functools.partial(_peer_kernel, tT=tT, EB=EB),
        grid=(T // tT, PEER_N // EB),
        in_specs=[pl.BlockSpec((tT, D_MODEL), row),
                  pl.BlockSpec((PEER_HEADS * PEER_DQ, D_MODEL), full2, pipeline_mode=pl.Buffered(1)),
                  pl.BlockSpec((2 * PEER_HEADS, PEER_NKEYS, PEER_DQ // 2), lambda i, j: (0, 0, 0)),
                  pl.BlockSpec((EB, D_MODEL), lambda i, j: (j, 0)),
                  pl.BlockSpec((D_MODEL, EB), lambda i, j: (0, j)),
                  pl.BlockSpec((1, D_MODEL), full2),
                  pl.BlockSpec((1, D_MODEL), full2)],
        out_specs=pl.BlockSpec((tT, D_MODEL), row),
        out_shape=jax.ShapeDtypeStruct((T, D_MODEL), F32),
        scratch_shapes=[pltpu.VMEM((tT, D_MODEL), BF16),
                        pltpu.VMEM((PEER_HEADS * PEER_DQ, tT), BF16),
                        pltpu.VMEM((2 * PEER_HEADS, nl, PEER_NKEYS, LANES), F32),
                        pltpu.VMEM(head_shape, BF16),
                        pltpu.VMEM(head_shape, BF16),
                        pltpu.VMEM(head_shape, jnp.uint32),
                        pltpu.VMEM(head_shape, jnp.uint32),
                        pltpu.SMEM((PEER_HEADS * nl,), jnp.int32),
                        pltpu.VMEM((EB, tT), F32),
                        pltpu.VMEM((EB, tT), BF16),
                        pltpu.VMEM((D_MODEL, tT), F32)],
        compiler_params=pltpu.CompilerParams(
            dimension_semantics=("parallel", "arbitrary"), vmem_limit_bytes=VMEM_LIMIT),
        name="peer_ln",
    )(h, wq_t, sk, u, v_t, g.reshape(1, -1), b.reshape(1, -1))


def _tile(n, pref):
    return pref if n % pref == 0 else n


def _run_layer(l, h, Bn, S, prefix, params, tabs, lam, last_meta=False):
    (w_in, conv_w, subln_g, w_out, ln1_g, ln1_b, wq_t, sk, u, v_t, ln2_g, ln2_b) = params
    lam_init = 0.8 - 0.6 * math.exp(-0.3 * l)
    is_meta = prefix is None
    t = _tile(S, 256)
    ts = _tile(S, 512)
    halo = jnp.zeros((8, CONV_WIDTH), F32) if is_meta else prefix[2]
    outs = _proj_conv(h, w_in, conv_w, halo, Bn, S, ts, emit_u=is_meta)
    q, k, v, c = outs[:4]
    new_prefix = (k, v, outs[4][N_META - 8:N_META]) if is_meta else None
    if last_meta:
        return None, new_prefix
    kp, vp = (k, v) if is_meta else prefix[:2]
    a = _diff_attention(q, k, v, kp, vp, tabs[0], tabs[1], lam, subln_g, Bn, S, t,
                        has_prefix=not is_meta, lam_init=lam_init)
    T = Bn * S
    h1 = _outproj_ln(a, c, h, w_out, ln1_g, ln1_b, _tile(T, 512))
    h2 = _peer_ln(h1, wq_t, sk, u, v_t, ln2_g, ln2_b, _tile(T, 512), PEER_BLOCK)
    return h2, new_prefix


def kernel(x, meta_tokens, ln_in_g, ln_in_b, rel_bias, w_in, conv_w, lambda_q1, lambda_k1, lambda_q2,
           lambda_k2, subln_g, w_out, ln1_g, ln1_b, peer_w_q, peer_sub_keys, peer_u, peer_v, ln2_g, ln2_b):
    Bn, S, _ = x.shape
    depth = w_in.shape[0]
    w_in_b = w_in.astype(BF16)
    w_out_b = w_out.astype(BF16)
    wq_t = jnp.swapaxes(peer_w_q, 1, 2).astype(BF16)
    sk = peer_sub_keys.reshape(depth, 2 * PEER_HEADS, PEER_NKEYS, PEER_DQ // 2).astype(BF16)
    u_b = peer_u.astype(BF16)
    v_t = jnp.swapaxes(peer_v, 1, 2).astype(BF16)

    t_main = _tile(S, 256)
    tabs_main = _bias_tables(rel_bias, t_main)
    tabs_meta = _bias_tables(rel_bias, META_ROWS)

    meta = jnp.zeros((META_ROWS, D_MODEL), F32).at[:N_META].set(meta_tokens.astype(F32))
    hm = _layer_norm_rows(meta, ln_in_g, ln_in_b, META_ROWS)
    h = _layer_norm_rows(x.reshape(Bn * S, D_MODEL), ln_in_g, ln_in_b, _tile(Bn * S, 512))

    for l in range(depth):
        lam_init = 0.8 - 0.6 * math.exp(-0.3 * l)
        lam = _lambda(lambda_q1[l], lambda_k1[l], lambda_q2[l], lambda_k2[l], lam_init)
        params = (w_in_b[l], conv_w[l], subln_g[l], w_out_b[l], ln1_g[l], ln1_b[l], wq_t[l], sk[l],
                  u_b[l], v_t[l], ln2_g[l], ln2_b[l])
        hm, prefix = _run_layer(l, hm, 1, META_ROWS, None, params, tabs_meta, lam,
                                last_meta=(l == depth - 1))
        h, _ = _run_layer(l, h, Bn, S, prefix, params, tabs_main, lam)
    return h.reshape(Bn, S, D_MODEL)
```

```python
import functools
import math

import jax
import jax.numpy as jnp
from jax import lax
from jax.experimental import pallas as pl
from jax.experimental.pallas import tpu as pltpu

F32 = jnp.float32
BF16 = jnp.bfloat16

D_MODEL = 1024
DEPTH = 4
N_META = 16
META_ROWS = 128
ATTN_HEADS = 4
ATTN_DK = 64
ATTN_DV = 2 * ATTN_DK
ATTN_WIDTH = ATTN_HEADS * ATTN_DV
CONV_WIDTH = D_MODEL - ATTN_WIDTH
CONV_K = 3
IN_COLS = 3 * ATTN_WIDTH + 3 * CONV_WIDTH
REL_BUCKETS = 32
REL_MAX_DIST = 128
PEER_HEADS = 8
PEER_NKEYS = 128
PEER_N = PEER_NKEYS * PEER_NKEYS
PEER_DQ = 256
PEER_TOPK = 16
DEEPNORM_ALPHA = (2 * DEPTH) ** 0.25
LN_EPS = 1e-5
RMS_EPS = 1e-5
NEG_BIG = -1e30
SQRT_HALF = math.sqrt(0.5)

PEER_BLOCK = 2048
LANES = 128
VMEM_LIMIT = 60000 * 1024

_NT = (((1,), (1,)), ((), ()))


def _ln(y, g, b):
    mu = jnp.mean(y, axis=-1, keepdims=True)
    yc = y - mu
    var = jnp.mean(yc * yc, axis=-1, keepdims=True)
    return yc * lax.rsqrt(var + LN_EPS) * g + b


def _ln_kernel(x_ref, g_ref, b_ref, o_ref):
    o_ref[...] = _ln(x_ref[...], g_ref[...], b_ref[...])


def _layer_norm_rows(x, g, b, tm):
    T = x.shape[0]
    return pl.pallas_call(
        _ln_kernel,
        grid=(T // tm,),
        in_specs=[pl.BlockSpec((tm, D_MODEL), lambda i: (i, 0)),
                  pl.BlockSpec((1, D_MODEL), lambda i: (0, 0)),
                  pl.BlockSpec((1, D_MODEL), lambda i: (0, 0))],
        out_specs=pl.BlockSpec((tm, D_MODEL), lambda i: (i, 0)),
        out_shape=jax.ShapeDtypeStruct((T, D_MODEL), F32),
        compiler_params=pltpu.CompilerParams(dimension_semantics=("parallel",)),
        name="ln_in",
    )(x, g.reshape(1, -1), b.reshape(1, -1))


def _proj_kernel(h_ref, w_ref, cw_ref, halo_ref, q_ref, k_ref, v_ref, c_ref, *rest, ts, emit_u):
    if emit_u:
        u_ref, ubuf = rest
    else:
        (ubuf,) = rest
    W = ATTN_WIDTH

    @pl.when(pl.program_id(1) == 0)
    def _():
        ubuf[0:8, :] = halo_ref[...]

    hb = h_ref[...].astype(BF16)

    def col(c):
        return jnp.dot(hb, w_ref[:, c * W:(c + 1) * W], preferred_element_type=F32)

    q_ref[...] = (col(0) * (ATTN_DK ** -0.5)).astype(BF16)
    k_ref[...] = col(1).astype(BF16)
    v_ref[...] = col(2).astype(BF16)
    u = col(4) * col(5)
    ubuf[8:8 + ts, :] = u
    if emit_u:
        u_ref[...] = u
    conv = (cw_ref[0:1, :] * ubuf[6:6 + ts, :] + cw_ref[1:2, :] * ubuf[7:7 + ts, :]
            + cw_ref[2:3, :] * u)
    c_ref[...] = (col(3) * conv).astype(BF16)
    ubuf[0:8, :] = ubuf[ts:ts + 8, :]


def _proj_conv(h, w_in, conv_w, halo, Bn, S, ts, emit_u):
    T = Bn * S
    nS = S // ts
    row = lambda b, s: (b * nS + s, 0)
    full = lambda b, s: (0, 0)
    out_shape = [jax.ShapeDtypeStruct((T, ATTN_WIDTH), BF16)] * 4
    out_specs = [pl.BlockSpec((ts, ATTN_WIDTH), row)] * 4
    if emit_u:
        out_shape = out_shape + [jax.ShapeDtypeStruct((T, CONV_WIDTH), F32)]
        out_specs = out_specs + [pl.BlockSpec((ts, CONV_WIDTH), row)]
    return pl.pallas_call(
        functools.partial(_proj_kernel, ts=ts, emit_u=emit_u),
        grid=(Bn, nS),
        in_specs=[pl.BlockSpec((ts, D_MODEL), row),
                  pl.BlockSpec((D_MODEL, IN_COLS), full),
                  pl.BlockSpec((CONV_K, CONV_WIDTH), full),
                  pl.BlockSpec((8, CONV_WIDTH), full)],
        out_specs=out_specs,
        out_shape=out_shape,
        scratch_shapes=[pltpu.VMEM((ts + 8, CONV_WIDTH), F32)],
        compiler_params=pltpu.CompilerParams(
            dimension_semantics=("parallel", "arbitrary"), vmem_limit_bytes=VMEM_LIMIT),
        name="proj_conv",
    )(h, w_in, conv_w, halo)


def _bias_kernel(rb_ref, d_ref, p_ref, *, t):
    h = pl.program_id(0)
    far = rb_ref[REL_BUCKETS - 1, h]

    def table(n):
        max_exact = REL_BUCKETS // 2
        nf = jnp.maximum(n, 1).astype(F32)
        large = max_exact + (jnp.log(nf / max_exact) / math.log(REL_MAX_DIST / max_exact)
                             * (REL_BUCKETS - max_exact)).astype(jnp.int32)
        large = jnp.minimum(large, REL_BUCKETS - 1)
        bucket = jnp.where(n < max_exact, n, large)
        out = jnp.zeros(n.shape, F32)
        for bkt in range(REL_BUCKETS):
            out = jnp.where(bucket == bkt, rb_ref[bkt, h], out)
        return out - far

    r = lax.broadcasted_iota(jnp.int32, (t, t), 0)
    c = lax.broadcasted_iota(jnp.int32, (t, t), 1)
    d_ref[0, 0] = jnp.where(c <= r, table(jnp.maximum(r - c, 0)), NEG_BIG)
    d_ref[0, 1] = table(t + r - c)
    rp = lax.broadcasted_iota(jnp.int32, (t, LANES), 0)
    cp = lax.broadcasted_iota(jnp.int32, (t, LANES), 1)
    p_ref[0, 0] = jnp.where(cp < N_META, table(N_META + rp - cp), NEG_BIG)
    p_ref[0, 1] = jnp.where(cp < N_META, 0.0, NEG_BIG)


def _bias_tables(rel_bias, t):
    return pl.pallas_call(
        functools.partial(_bias_kernel, t=t),
        grid=(ATTN_HEADS,),
        in_specs=[pl.BlockSpec(memory_space=pltpu.SMEM)],
        out_specs=[pl.BlockSpec((1, 2, t, t), lambda h: (h, 0, 0, 0)),
                   pl.BlockSpec((1, 2, t, LANES), lambda h: (h, 0, 0, 0))],
        out_shape=[jax.ShapeDtypeStruct((ATTN_HEADS, 2, t, t), F32),
                   jax.ShapeDtypeStruct((ATTN_HEADS, 2, t, LANES), F32)],
        name="bias_tables",
    )(rel_bias)


def _attn_kernel(q_ref, k_ref, v_ref, kp_ref, vp_ref, d_ref, p_ref, lam_ref, g_ref, o_ref,
                 s_ref, sp_ref, m_ref, l_ref, acc_ref, *, t, has_prefix, lam_init):
    i = pl.program_id(1)
    nc = t // LANES
    hm = [(h, m) for h in range(ATTN_HEADS) for m in range(2)]

    def cols(h, m):
        c0 = h * ATTN_DV + m * ATTN_DK
        return slice(c0, c0 + ATTN_DK)

    def vcols(h):
        return slice(h * ATTN_DV, (h + 1) * ATTN_DV)

    def lane_fold(x, op):
        out = x[:, :LANES]
        for c in range(1, x.shape[1] // LANES):
            out = op(out, x[:, c * LANES:(c + 1) * LANES])
        return out

    def logits(h, m, k_tile_ref, rows):
        return lax.dot_general(q_ref[:, cols(h, m)], k_tile_ref[rows, cols(h, m)], _NT,
                               preferred_element_type=F32)

    if has_prefix:
        first = jnp.minimum(i, 1)
        for h, m in hm:
            sp = logits(h, m, kp_ref, slice(None)) + p_ref[h, first]
            sp_ref[h, m] = sp
            m_ref[h, m] = sp
    else:
        m_ref[...] = jnp.full(m_ref.shape, -jnp.inf, F32)

    def score_tile(j, near):
        rows = pl.ds(pl.multiple_of(j * t, t), t)
        for h, m in hm:
            s = logits(h, m, k_ref, rows)
            if near:
                s = s + d_ref[h, i - j]
            s_ref[h, m, j] = s
            m_ref[h, m] = jnp.maximum(m_ref[h, m], lane_fold(s, jnp.maximum))

    def far_body(j, c):
        score_tile(j, False)
        return c

    def near_body(j, c):
        score_tile(j, True)
        return c

    lax.fori_loop(0, jnp.maximum(i - 1, 0), far_body, 0)
    lax.fori_loop(jnp.maximum(i - 1, 0), i + 1, near_body, 0)

    for h, m in hm:
        m_ref[h, m] = jnp.broadcast_to(jnp.max(m_ref[h, m], axis=-1, keepdims=True), (t, LANES))

    if has_prefix:
        for h, m in hm:
            p = jnp.exp(sp_ref[h, m] - m_ref[h, m])
            l_ref[h, m] = p
            acc_ref[h, m] = jnp.dot(p.astype(BF16), vp_ref[:, vcols(h)], preferred_element_type=F32)
    else:
        l_ref[...] = jnp.zeros_like(l_ref)
        acc_ref[...] = jnp.zeros_like(acc_ref)

    def value_body(j, c):
        rows = pl.ds(pl.multiple_of(j * t, t), t)
        for h, m in hm:
            p = jnp.exp(s_ref[h, m, j] - jnp.concatenate([m_ref[h, m]] * nc, axis=1))
            l_ref[h, m] += lane_fold(p, jnp.add)
            acc_ref[h, m] += jnp.dot(p.astype(BF16), v_ref[rows, vcols(h)], preferred_element_type=F32)
        return c

    lax.fori_loop(0, i + 1, value_body, 0)

    lam = lam_ref[0, 0]
    for h in range(ATTN_HEADS):
        o = [acc_ref[h, m] / jnp.sum(l_ref[h, m], axis=-1, keepdims=True) for m in range(2)]
        out = o[0] - lam * o[1]
        out = out * lax.rsqrt(jnp.mean(out * out, axis=-1, keepdims=True) + RMS_EPS)
        o_ref[:, vcols(h)] = (out * g_ref[...] * (1.0 - lam_init)).astype(BF16)


def _diff_attention(q, k, v, kp, vp, dtab, ptab, lam, subln_g, Bn, S, t, has_prefix, lam_init):
    nq = S // t
    W = ATTN_WIDTH
    hm_shape = (ATTN_HEADS, 2, t, LANES)
    return pl.pallas_call(
        functools.partial(_attn_kernel, t=t, has_prefix=has_prefix, lam_init=lam_init),
        grid=(Bn, nq),
        in_specs=[pl.BlockSpec((t, W), lambda b, i: (b * nq + i, 0)),
                  pl.BlockSpec((S, W), lambda b, i: (b, 0)),
                  pl.BlockSpec((S, W), lambda b, i: (b, 0)),
                  pl.BlockSpec((META_ROWS, W), lambda b, i: (0, 0)),
                  pl.BlockSpec((META_ROWS, W), lambda b, i: (0, 0)),
                  pl.BlockSpec((ATTN_HEADS, 2, t, t), lambda b, i: (0, 0, 0, 0)),
                  pl.BlockSpec((ATTN_HEADS, 2, t, LANES), lambda b, i: (0, 0, 0, 0)),
                  pl.BlockSpec(memory_space=pltpu.SMEM),
                  pl.BlockSpec((1, ATTN_DV), lambda b, i: (0, 0))],
        out_specs=pl.BlockSpec((t, W), lambda b, i: (b * nq + i, 0)),
        out_shape=jax.ShapeDtypeStruct((Bn * S, W), BF16),
        scratch_shapes=[pltpu.VMEM((ATTN_HEADS, 2, nq, t, t), F32),
                        pltpu.VMEM(hm_shape, F32),
                        pltpu.VMEM(hm_shape, F32),
                        pltpu.VMEM(hm_shape, F32),
                        pltpu.VMEM(hm_shape, F32)],
        compiler_params=pltpu.CompilerParams(
            dimension_semantics=("parallel", "arbitrary"), vmem_limit_bytes=VMEM_LIMIT),
        name="diff_attn",
    )(q, k, v, kp, vp, dtab, ptab, lam, subln_g.reshape(1, -1))


def _lam_kernel(q1_ref, k1_ref, q2_ref, k2_ref, o_ref, *, lam_init):
    a = jnp.sum(q1_ref[...] * k1_ref[...], axis=-1, keepdims=True)
    b = jnp.sum(q2_ref[...] * k2_ref[...], axis=-1, keepdims=True)
    o_ref[...] = jnp.broadcast_to(jnp.exp(a) - jnp.exp(b) + lam_init, o_ref.shape)


def _lambda(q1, k1, q2, k2, lam_init):
    r = lambda x: x.reshape(1, ATTN_DK)
    return pl.pallas_call(
        functools.partial(_lam_kernel, lam_init=lam_init),
        out_shape=jax.ShapeDtypeStruct((1, LANES), F32),
        name="lambda",
    )(r(q1), r(k1), r(q2), r(k2))


def _outproj_kernel(a_ref, c_ref, h_ref, w_ref, g_ref, b_ref, o_ref):
    mix = jnp.dot(a_ref[...], w_ref[0:ATTN_WIDTH, :], preferred_element_type=F32)
    mix = mix + jnp.dot(c_ref[...], w_ref[ATTN_WIDTH:, :], preferred_element_type=F32)
    o_ref[...] = _ln(DEEPNORM_ALPHA * h_ref[...] + mix, g_ref[...], b_ref[...])


def _outproj_ln(a, c, h, w_out, g, b, tm):
    T = h.shape[0]
    row = lambda i: (i, 0)
    full = lambda i: (0, 0)
    return pl.pallas_call(
        _outproj_kernel,
        grid=(T // tm,),
        in_specs=[pl.BlockSpec((tm, ATTN_WIDTH), row),
                  pl.BlockSpec((tm, CONV_WIDTH), row),
                  pl.BlockSpec((tm, D_MODEL), row),
                  pl.BlockSpec((D_MODEL, D_MODEL), full),
                  pl.BlockSpec((1, D_MODEL), full),
                  pl.BlockSpec((1, D_MODEL), full)],
        out_specs=pl.BlockSpec((tm, D_MODEL), row),
        out_shape=jax.ShapeDtypeStruct((T, D_MODEL), F32),
        compiler_params=pltpu.CompilerParams(
            dimension_semantics=("parallel",), vmem_limit_bytes=VMEM_LIMIT),
        name="outproj_ln",
    )(a, c, h, w_out, g.reshape(1, -1), b.reshape(1, -1))


NOT_SELECTED = 64.0


def _top16_ordered(s):
    vals = []
    cur = s
    rank = jnp.full(s.shape, NOT_SELECTED, F32)
    rows = lax.broadcasted_iota(jnp.int32, s.shape, 0).astype(F32)
    for i in range(PEER_TOPK):
        m = jnp.max(cur, axis=0, keepdims=True)
        vals.append(m)
        first = jnp.min(jnp.where(cur == m, rows, float(PEER_NKEYS)), axis=0, keepdims=True)
        hit = rows == first
        rank = jnp.where(hit, float(i), rank)
        cur = jnp.where(hit, -jnp.inf, cur)
    return vals, rank


def _oddeven_merge_sort_pairs(n):
    pairs = []

    def merge(lo, hi, r):
        step = r * 2
        if step < hi - lo:
            merge(lo, hi, step)
            merge(lo + r, hi, step)
            pairs.extend((i, i + r) for i in range(lo + r, hi - r, step))
        else:
            pairs.append((lo, lo + r))

    def sort(lo, hi):
        if hi - lo >= 1:
            mid = lo + (hi - lo) // 2
            sort(lo, mid)
            sort(mid + 1, hi)
            merge(lo, hi, 1)

    sort(0, n - 1)
    return pairs


_SORT16 = _oddeven_merge_sort_pairs(PEER_TOPK)


def _leaders(s):
    x = [s[8 * i:8 * i + 8, :] for i in range(s.shape[0] // 8)]
    x = x + [jnp.full((8, LANES), -jnp.inf, F32)] * (PEER_TOPK - len(x))

    def exchange(x, i, j):
        x[i], x[j] = jnp.maximum(x[i], x[j]), jnp.minimum(x[i], x[j])

    for i, j in _SORT16:
        exchange(x, i, j)
    for shift in (4, 2, 1):
        y = [pltpu.roll(v, shift, axis=0) for v in x]
        x = [jnp.maximum(x[i], y[PEER_TOPK - 1 - i]) for i in range(PEER_TOPK)]
        for d in (8, 4, 2, 1):
            for i in range(PEER_TOPK):
                if i & d == 0:
                    exchange(x, i, i + d)
    return [v[0:1, :] for v in x]


def _dup_bf16(x):
    b = pltpu.bitcast(x, jnp.uint32)
    hi = (b + jnp.uint32(0x7FFF) + ((b >> 16) & jnp.uint32(1))) & jnp.uint32(0xFFFF0000)
    return hi | (hi >> 16)


def _pair_candidates(c1, c2):
    c1s = jnp.concatenate(c1, axis=0)
    c2s = jnp.concatenate(c2, axis=0)
    lo = c2s[0:8, :]
    row = lax.broadcasted_iota(jnp.int32, lo.shape, 0)
    parts = [c1s + c2[0], c1s[0:8, :] + c2[1], c1s[0:8, :] + c2[2], c1s[0:8, :] + c2[3],
             c1[0] + c2s[8:16, :]]
    for i in range(3):
        parts.append(jnp.where(row >= 4, c1[i] + lo, -jnp.inf))
    return jnp.concatenate(parts, axis=0), c2s


def _gate_tables(s1, s2, c1, c2, rank2, count, z):
    e1 = jnp.exp(s1 - c1[0]) * (0.5 / z)
    e2 = jnp.exp(s2 - c2[0])
    return rank2.astype(BF16), e2.astype(BF16), _dup_bf16(count), _dup_bf16(e1)


def _route_distinct(s1, s2, c1, c2, rank2):
    cand, c2s = _pair_candidates(c1, c2)
    tau = _leaders(cand)[PEER_TOPK - 1]
    removed = jnp.sum(jnp.where(cand >= tau, 1.0, 0.0), axis=0, keepdims=True)
    z = jnp.sum(jnp.where(cand >= tau, jnp.exp(cand - (c1[0] + c2[0])), 0.0), axis=0, keepdims=True)
    count = jnp.zeros(s1.shape, F32)
    for i in range(PEER_TOPK):
        n = jnp.sum(jnp.where(c1[i] + c2s >= tau, 1.0, 0.0), axis=0, keepdims=True)
        count = jnp.where(s1 == c1[i], n, count)
    return _gate_tables(s1, s2, c1, c2, rank2, count, z), removed


def _route_ties(s1, s2):
    c1, rank1 = _top16_ordered(s1)
    c2, rank2 = _top16_ordered(s2)
    cand, c2s = _pair_candidates(c1, c2)
    top = c1[0] + c2[0]
    cur = cand
    tau = top
    got = jnp.zeros_like(top)
    above = got
    for _ in range(PEER_TOPK):
        m = jnp.max(cur, axis=0, keepdims=True)
        hit = cur == m
        filling = got < float(PEER_TOPK)
        tau = jnp.where(filling, m, tau)
        above = jnp.where(filling, got, above)
        got = got + jnp.sum(jnp.where(hit, 1.0, 0.0), axis=0, keepdims=True)
        cur = jnp.where(hit, -jnp.inf, cur)
    at_tau = float(PEER_TOPK) - above
    z = jnp.sum(jnp.where(cand > tau, jnp.exp(cand - top), 0.0), axis=0, keepdims=True)
    z = z + at_tau * jnp.exp(tau - top)
    count = jnp.zeros(s1.shape, F32)
    used = jnp.zeros_like(tau)
    for i in range(PEER_TOPK):
        sums = c1[i] + c2s
        n_gt = jnp.sum(jnp.where(sums > tau, 1.0, 0.0), axis=0, keepdims=True)
        n_eq = jnp.sum(jnp.where(sums == tau, 1.0, 0.0), axis=0, keepdims=True)
        take = jnp.minimum(n_eq, jnp.maximum(at_tau - used, 0.0))
        used = used + take
        count = jnp.where(rank1 == float(i), n_gt + take, count)
    return _gate_tables(s1, s2, c1, c2, rank2, count, z)


def _select(s1, s2):
    c1 = _leaders(s1)
    c2 = _leaders(s2)
    rank2 = jnp.full(s2.shape, NOT_SELECTED, F32)
    for i in range(PEER_TOPK):
        rank2 = jnp.where(s2 == c2[i], float(i), rank2)
    k = float(PEER_TOPK)
    n1 = jnp.sum(jnp.where(s1 >= c1[PEER_TOPK - 1], 1.0, 0.0), axis=0, keepdims=True)
    n2 = jnp.sum(jnp.where(rank2 < NOT_SELECTED, 1.0, 0.0), axis=0, keepdims=True)
    dup = jnp.zeros_like(n1)
    for i in range(PEER_TOPK - 1):
        dup = dup + jnp.where(c1[i] == c1[i + 1], 1.0, 0.0) + jnp.where(c2[i] == c2[i + 1], 1.0, 0.0)
    tables, removed3 = _route_distinct(s1, s2, c1, c2, rank2)
    tied = jnp.max(jnp.abs(n1 - k) + jnp.abs(n2 - k) + dup + jnp.abs(removed3 - k)) > 0.0
    return tables, tied


def _peer_kernel(h_ref, wq_ref, sk_ref, u_ref, vt_ref, g_ref, b_ref, o_ref,
                 hb_ref, qp_ref, s_ref, r2_ref, e2_ref, n1_ref, e1_ref, tied_ref, act_ref, wact_ref, acc_ref,
                 *, tT, EB):
    j = pl.program_id(1)
    nl = tT // LANES
    na = EB // PEER_NKEYS
    RB = 16
    AG = 4
    nr = PEER_NKEYS // RB

    @pl.when(j == 0)
    def _():
        hb = h_ref[...].astype(BF16)
        hb_ref[...] = hb
        QC = 512
        for c in range(PEER_HEADS * PEER_DQ // QC):
            qp_ref[c * QC:(c + 1) * QC, :] = lax.dot_general(
                wq_ref[c * QC:(c + 1) * QC, :], hb, _NT, preferred_element_type=F32).astype(BF16)
        half = PEER_DQ // 2
        for hp in range(2 * PEER_HEADS):
            sT = jnp.dot(sk_ref[hp], qp_ref[hp * half:(hp + 1) * half, :], preferred_element_type=F32)
            for l in range(nl):
                s_ref[hp, l] = sT[:, l * LANES:(l + 1) * LANES]

        def select(idx, carry):
            h = idx // (nl // lg)
            l0 = (idx % (nl // lg)) * lg
            for dl in range(lg):
                l = l0 + dl
                tables, tied = _select(s_ref[2 * h, l], s_ref[2 * h + 1, l])
                r2_ref[h, l], e2_ref[h, l], n1_ref[h, l], e1_ref[h, l] = tables
                tied_ref[h * nl + l] = tied.astype(jnp.int32)
            return carry

        lg = 2 if nl % 2 == 0 else 1
        lax.fori_loop(0, PEER_HEADS * nl // lg, select, 0)

        def redo_ties(idx, carry):
            @pl.when(tied_ref[idx] != 0)
            def _():
                h = idx // nl
                l = idx % nl
                r2_ref[h, l], e2_ref[h, l], n1_ref[h, l], e1_ref[h, l] = _route_ties(s_ref[2 * h, l], s_ref[2 * h + 1, l])
            return carry

        lax.fori_loop(0, PEER_HEADS * nl, redo_ties, 0)
        acc_ref[...] = jnp.zeros_like(acc_ref)

    act_ref[...] = lax.dot_general(u_ref[...], hb_ref[...], _NT, preferred_element_type=F32)

    def row_bf16(ref, h, l, a):
        return pltpu.bitcast(jnp.broadcast_to(ref[h, l, pl.ds(a, 1), :], (8, LANES)), BF16)

    zero = jnp.zeros((), BF16)

    def gate_rows(t, carry):
        for l in range(nl):
            lanes = slice(l * LANES, (l + 1) * LANES)
            for aa in range(AG):
                al = t * AG + aa
                a = j * na + al
                w = [jnp.zeros((RB, LANES), BF16)] * nr
                for h in range(PEER_HEADS):
                    n1r = row_bf16(n1_ref, h, l, a)
                    e1r = row_bf16(e1_ref, h, l, a)
                    for r in range(nr):
                        r2c = r2_ref[h, l, r * RB:(r + 1) * RB, :]
                        e2c = e2_ref[h, l, r * RB:(r + 1) * RB, :]
                        w[r] = w[r] + jnp.minimum(e1r * e2c, jnp.maximum(n1r - r2c, zero))
                for r in range(nr):
                    rows = pl.ds(pl.multiple_of(al * PEER_NKEYS + r * RB, RB), RB)
                    x = act_ref[rows, lanes]
                    gelu2 = x * (1.0 + lax.erf(x * SQRT_HALF))
                    wact_ref[rows, lanes] = w[r] * gelu2.astype(BF16)
        return carry

    lax.fori_loop(0, na // AG, gate_rows, 0)
    acc_ref[...] += jnp.dot(vt_ref[...], wact_ref[...], preferred_element_type=F32)

    @pl.when(j == pl.num_programs(1) - 1)
    def _():
        y = DEEPNORM_ALPHA * h_ref[...] + acc_ref[...].T
        o_ref[...] = _ln(y, g_ref[...], b_ref[...])


def _peer_ln(h, wq_t, sk, u, v_t, g, b, tT, EB):
    T = h.shape[0]
    nl = tT // LANES
    row = lambda i, j: (i, 0)
    full2 = lambda i, j: (0, 0)
    head_shape = (PEER_HEADS, nl, PEER_NKEYS, LANES)
    return pl.pallas_call(
        functools.partial(_peer_kernel, tT=tT, EB=EB),
        grid=(T // tT, PEER_N // EB),
        in_specs=[pl.BlockSpec((tT, D_MODEL), row),
                  pl.BlockSpec((PEER_HEADS * PEER_DQ, D_MODEL), full2, pipeline_mode=pl.Buffered(1)),
                  pl.BlockSpec((2 * PEER_HEADS, PEER_NKEYS, PEER_DQ // 2), lambda i, j: (0, 0, 0)),
                  pl.BlockSpec((EB, D_MODEL), lambda i, j: (j, 0)),
                  pl.BlockSpec((D_MODEL, EB), lambda i, j: (0, j)),
                  pl.BlockSpec((1, D_MODEL), full2),
                  pl.BlockSpec((1, D_MODEL), full2)],
        out_specs=pl.BlockSpec((tT, D_MODEL), row),
        out_shape=jax.ShapeDtypeStruct((T, D_MODEL), F32),
        scratch_shapes=[pltpu.VMEM((tT, D_MODEL), BF16),
                        pltpu.VMEM((PEER_HEADS * PEER_DQ, tT), BF16),
                        pltpu.VMEM((2 * PEER_HEADS, nl, PEER_NKEYS, LANES), F32),
                        pltpu.VMEM(head_shape, BF16),
                        pltpu.VMEM(head_shape, BF16),
                        pltpu.VMEM(head_shape, jnp.uint32),
                        pltpu.VMEM(head_shape, jnp.uint32),
                        pltpu.SMEM((PEER_HEADS * nl,), jnp.int32),
                        pltpu.VMEM((EB, tT), F32),
                        pltpu.VMEM((EB, tT), BF16),
                        pltpu.VMEM((D_MODEL, tT), F32)],
        compiler_params=pltpu.CompilerParams(
            dimension_semantics=("parallel", "arbitrary"), vmem_limit_bytes=VMEM_LIMIT),
        name="peer_ln",
    )(h, wq_t, sk, u, v_t, g.reshape(1, -1), b.reshape(1, -1))


def _tile(n, pref):
    return pref if n % pref == 0 else n


def _run_layer(l, h, Bn, S, prefix, params, tabs, lam, last_meta=False):
    (w_in, conv_w, subln_g, w_out, ln1_g, ln1_b, wq_t, sk, u, v_t, ln2_g, ln2_b) = params
    lam_init = 0.8 - 0.6 * math.exp(-0.3 * l)
    is_meta = prefix is None
    t = _tile(S, 256)
    ts = _tile(S, 512)
    halo = jnp.zeros((8, CONV_WIDTH), F32) if is_meta else prefix[2]
    outs = _proj_conv(h, w_in, conv_w, halo, Bn, S, ts, emit_u=is_meta)
    q, k, v, c = outs[:4]
    new_prefix = (k, v, outs[4][N_META - 8:N_META]) if is_meta else None
    if last_meta:
        return None, new_prefix
    kp, vp = (k, v) if is_meta else prefix[:2]
    a = _diff_attention(q, k, v, kp, vp, tabs[0], tabs[1], lam, subln_g, Bn, S, t,
                        has_prefix=not is_meta, lam_init=lam_init)
    T = Bn * S
    h1 = _outproj_ln(a, c, h, w_out, ln1_g, ln1_b, _tile(T, 512))
    h2 = _peer_ln(h1, wq_t, sk, u, v_t, ln2_g, ln2_b, _tile(T, 512), PEER_BLOCK)
    return h2, new_prefix


def kernel(x, meta_tokens, ln_in_g, ln_in_b, rel_bias, w_in, conv_w, lambda_q1, lambda_k1, lambda_q2,
           lambda_k2, subln_g, w_out, ln1_g, ln1_b, peer_w_q, peer_sub_keys, peer_u, peer_v, ln2_g, ln2_b):
    Bn, S, _ = x.shape
    depth = w_in.shape[0]
    w_in_b = w_in.astype(BF16)
    w_out_b = w_out.astype(BF16)
    wq_t = jnp.swapaxes(peer_w_q, 1, 2).astype(BF16)
    sk = peer_sub_keys.reshape(depth, 2 * PEER_HEADS, PEER_NKEYS, PEER_DQ // 2).astype(BF16)
    u_b = peer_u.astype(BF16)
    v_t = jnp.swapaxes(peer_v, 1, 2).astype(BF16)

    t_main = _tile(S, 256)
    tabs_main = _bias_tables(rel_bias, t_main)
    tabs_meta = _bias_tables(rel_bias, META_ROWS)

    meta = jnp.zeros((META_ROWS, D_MODEL), F32).at[:N_META].set(meta_tokens.astype(F32))
    hm = _layer_norm_rows(meta, ln_in_g, ln_in_b, META_ROWS)
    h = _layer_norm_rows(x.reshape(Bn * S, D_MODEL), ln_in_g, ln_in_b, _tile(Bn * S, 512))

    for l in range(depth):
        lam_init = 0.8 - 0.6 * math.exp(-0.3 * l)
        lam = _lambda(lambda_q1[l], lambda_k1[l], lambda_q2[l], lambda_k2[l], lam_init)
        params = (w_in_b[l], conv_w[l], subln_g[l], w_out_b[l], ln1_g[l], ln1_b[l], wq_t[l], sk[l],
                  u_b[l], v_t[l], ln2_g[l], ln2_b[l])
        hm, prefix = _run_layer(l, hm, 1, META_ROWS, None, params, tabs_meta, lam,
                                last_meta=(l == depth - 1))
        h, _ = _run_layer(l, h, Bn, S, prefix, params, tabs_main, lam)
    return h.reshape(Bn, S, D_MODEL)
```

```python
import functools
import math

import jax
import jax.numpy as jnp
from jax import lax
from jax.experimental import pallas as pl
from jax.experimental.pallas import tpu as pltpu

F32 = jnp.float32
BF16 = jnp.bfloat16

D_MODEL = 1024
DEPTH = 4
N_META = 16
META_ROWS = 128
ATTN_HEADS = 4
ATTN_DK = 64
ATTN_DV = 2 * ATTN_DK
ATTN_WIDTH = ATTN_HEADS * ATTN_DV
CONV_WIDTH = D_MODEL - ATTN_WIDTH
CONV_K = 3
IN_COLS = 3 * ATTN_WIDTH + 3 * CONV_WIDTH
REL_BUCKETS = 32
REL_MAX_DIST = 128
PEER_HEADS = 8
PEER_NKEYS = 128
PEER_N = PEER_NKEYS * PEER_NKEYS
PEER_DQ = 256
PEER_TOPK = 16
DEEPNORM_ALPHA = (2 * DEPTH) ** 0.25
LN_EPS = 1e-5
RMS_EPS = 1e-5
NEG_BIG = -1e30
SQRT_HALF = math.sqrt(0.5)

PEER_BLOCK = 2048
LANES = 128
VMEM_LIMIT = 60000 * 1024

_NT = (((1,), (1,)), ((), ()))


def _ln(y, g, b):
    mu = jnp.mean(y, axis=-1, keepdims=True)
    yc = y - mu
    var = jnp.mean(yc * yc, axis=-1, keepdims=True)
    return yc * lax.rsqrt(var + LN_EPS) * g + b


def _ln_kernel(x_ref, g_ref, b_ref, o_ref):
    o_ref[...] = _ln(x_ref[...], g_ref[...], b_ref[...])


def _layer_norm_rows(x, g, b, tm):
    T = x.shape[0]
    return pl.pallas_call(
        _ln_kernel,
        grid=(T // tm,),
        in_specs=[pl.BlockSpec((tm, D_MODEL), lambda i: (i, 0)),
                  pl.BlockSpec((1, D_MODEL), lambda i: (0, 0)),
                  pl.BlockSpec((1, D_MODEL), lambda i: (0, 0))],
        out_specs=pl.BlockSpec((tm, D_MODEL), lambda i: (i, 0)),
        out_shape=jax.ShapeDtypeStruct((T, D_MODEL), F32),
        compiler_params=pltpu.CompilerParams(dimension_semantics=("parallel",)),
        name="ln_in",
    )(x, g.reshape(1, -1), b.reshape(1, -1))


def _proj_kernel(h_ref, w_ref, cw_ref, halo_ref, q_ref, k_ref, v_ref, c_ref, *rest, ts, emit_u):
    if emit_u:
        u_ref, ubuf = rest
    else:
        (ubuf,) = rest
    W = ATTN_WIDTH

    @pl.when(pl.program_id(1) == 0)
    def _():
        ubuf[0:8, :] = halo_ref[...]

    hb = h_ref[...].astype(BF16)

    def col(c):
        return jnp.dot(hb, w_ref[:, c * W:(c + 1) * W], preferred_element_type=F32)

    q_ref[...] = (col(0) * (ATTN_DK ** -0.5)).astype(BF16)
    k_ref[...] = col(1).astype(BF16)
    v_ref[...] = col(2).astype(BF16)
    u = col(4) * col(5)
    ubuf[8:8 + ts, :] = u
    if emit_u:
        u_ref[...] = u
    conv = (cw_ref[0:1, :] * ubuf[6:6 + ts, :] + cw_ref[1:2, :] * ubuf[7:7 + ts, :]
            + cw_ref[2:3, :] * u)
    c_ref[...] = (col(3) * conv).astype(BF16)
    ubuf[0:8, :] = ubuf[ts:ts + 8, :]


def _proj_conv(h, w_in, conv_w, halo, Bn, S, ts, emit_u):
    T = Bn * S
    nS = S // ts
    row = lambda b, s: (b * nS + s, 0)
    full = lambda b, s: (0, 0)
    out_shape = [jax.ShapeDtypeStruct((T, ATTN_WIDTH), BF16)] * 4
    out_specs = [pl.BlockSpec((ts, ATTN_WIDTH), row)] * 4
    if emit_u:
        out_shape = out_shape + [jax.ShapeDtypeStruct((T, CONV_WIDTH), F32)]
        out_specs = out_specs + [pl.BlockSpec((ts, CONV_WIDTH), row)]
    return pl.pallas_call(
        functools.partial(_proj_kernel, ts=ts, emit_u=emit_u),
        grid=(Bn, nS),
        in_specs=[pl.BlockSpec((ts, D_MODEL), row),
                  pl.BlockSpec((D_MODEL, IN_COLS), full),
                  pl.BlockSpec((CONV_K, CONV_WIDTH), full),
                  pl.BlockSpec((8, CONV_WIDTH), full)],
        out_specs=out_specs,
        out_shape=out_shape,
        scratch_shapes=[pltpu.VMEM((ts + 8, CONV_WIDTH), F32)],
        compiler_params=pltpu.CompilerParams(
            dimension_semantics=("parallel", "arbitrary"), vmem_limit_bytes=VMEM_LIMIT),
        name="proj_conv",
    )(h, w_in, conv_w, halo)


def _bias_kernel(rb_ref, d_ref, p_ref, *, t):
    h = pl.program_id(0)
    far = rb_ref[REL_BUCKETS - 1, h]

    def table(n):
        max_exact = REL_BUCKETS // 2
        nf = jnp.maximum(n, 1).astype(F32)
        large = max_exact + (jnp.log(nf / max_exact) / math.log(REL_MAX_DIST / max_exact)
                             * (REL_BUCKETS - max_exact)).astype(jnp.int32)
        large = jnp.minimum(large, REL_BUCKETS - 1)
        bucket = jnp.where(n < max_exact, n, large)
        out = jnp.zeros(n.shape, F32)
        for bkt in range(REL_BUCKETS):
            out = jnp.where(bucket == bkt, rb_ref[bkt, h], out)
        return out - far

    r = lax.broadcasted_iota(jnp.int32, (t, t), 0)
    c = lax.broadcasted_iota(jnp.int32, (t, t), 1)
    d_ref[0, 0] = jnp.where(c <= r, table(jnp.maximum(r - c, 0)), NEG_BIG)
    d_ref[0, 1] = table(t + r - c)
    rp = lax.broadcasted_iota(jnp.int32, (t, LANES), 0)
    cp = lax.broadcasted_iota(jnp.int32, (t, LANES), 1)
    p_ref[0, 0] = jnp.where(cp < N_META, table(N_META + rp - cp), NEG_BIG)
    p_ref[0, 1] = jnp.where(cp < N_META, 0.0, NEG_BIG)


def _bias_tables(rel_bias, t):
    return pl.pallas_call(
        functools.partial(_bias_kernel, t=t),
        grid=(ATTN_HEADS,),
        in_specs=[pl.BlockSpec(memory_space=pltpu.SMEM)],
        out_specs=[pl.BlockSpec((1, 2, t, t), lambda h: (h, 0, 0, 0)),
                   pl.BlockSpec((1, 2, t, LANES), lambda h: (h, 0, 0, 0))],
        out_shape=[jax.ShapeDtypeStruct((ATTN_HEADS, 2, t, t), F32),
                   jax.ShapeDtypeStruct((ATTN_HEADS, 2, t, LANES), F32)],
        name="bias_tables",
    )(rel_bias)


def _attn_kernel(q_ref, k_ref, v_ref, kp_ref, vp_ref, d_ref, p_ref, lam_ref, g_ref, o_ref,
                 s_ref, sp_ref, m_ref, l_ref, acc_ref, *, t, has_prefix, lam_init):
    i = pl.program_id(1)
    nc = t // LANES
    hm = [(h, m) for h in range(ATTN_HEADS) for m in range(2)]

    def cols(h, m):
        c0 = h * ATTN_DV + m * ATTN_DK
        return slice(c0, c0 + ATTN_DK)

    def vcols(h):
        return slice(h * ATTN_DV, (h + 1) * ATTN_DV)

    def lane_fold(x, op):
        out = x[:, :LANES]
        for c in range(1, x.shape[1] // LANES):
            out = op(out, x[:, c * LANES:(c + 1) * LANES])
        return out

    def logits(h, m, k_tile_ref, rows):
        return lax.dot_general(q_ref[:, cols(h, m)], k_tile_ref[rows, cols(h, m)], _NT,
                               preferred_element_type=F32)

    if has_prefix:
        first = jnp.minimum(i, 1)
        for h, m in hm:
            sp = logits(h, m, kp_ref, slice(None)) + p_ref[h, first]
            sp_ref[h, m] = sp
            m_ref[h, m] = sp
    else:
        m_ref[...] = jnp.full(m_ref.shape, -jnp.inf, F32)

    def score_tile(j, near):
        rows = pl.ds(pl.multiple_of(j * t, t), t)
        for h, m in hm:
            s = logits(h, m, k_ref, rows)
            if near:
                s = s + d_ref[h, i - j]
            s_ref[h, m, j] = s
            m_ref[h, m] = jnp.maximum(m_ref[h, m], lane_fold(s, jnp.maximum))

    def far_body(j, c):
        score_tile(j, False)
        return c

    def near_body(j, c):
        score_tile(j, True)
        return c

    lax.fori_loop(0, jnp.maximum(i - 1, 0), far_body, 0)
    lax.fori_loop(jnp.maximum(i - 1, 0), i + 1, near_body, 0)

    for h, m in hm:
        m_ref[h, m] = jnp.broadcast_to(jnp.max(m_ref[h, m], axis=-1, keepdims=True), (t, LANES))

    if has_prefix:
        for h, m in hm:
            p = jnp.exp(sp_ref[h, m] - m_ref[h, m])
            l_ref[h, m] = p
            acc_ref[h, m] = jnp.dot(p.astype(BF16), vp_ref[:, vcols(h)], preferred_element_type=F32)
    else:
        l_ref[...] = jnp.zeros_like(l_ref)
        acc_ref[...] = jnp.zeros_like(acc_ref)

    def value_body(j, c):
        rows = pl.ds(pl.multiple_of(j * t, t), t)
        for h, m in hm:
            p = jnp.exp(s_ref[h, m, j] - jnp.concatenate([m_ref[h, m]] * nc, axis=1))
            l_ref[h, m] += lane_fold(p, jnp.add)
            acc_ref[h, m] += jnp.dot(p.astype(BF16), v_ref[rows, vcols(h)], preferred_element_type=F32)
        return c

    lax.fori_loop(0, i + 1, value_body, 0)

    lam = lam_ref[0, 0]
    for h in range(ATTN_HEADS):
        o = [acc_ref[h, m] / jnp.sum(l_ref[h, m], axis=-1, keepdims=True) for m in range(2)]
        out = o[0] - lam * o[1]
        out = out * lax.rsqrt(jnp.mean(out * out, axis=-1, keepdims=True) + RMS_EPS)
        o_ref[:, vcols(h)] = (out * g_ref[...] * (1.0 - lam_init)).astype(BF16)


def _diff_attention(q, k, v, kp, vp, dtab, ptab, lam, subln_g, Bn, S, t, has_prefix, lam_init):
    nq = S // t
    W = ATTN_WIDTH
    hm_shape = (ATTN_HEADS, 2, t, LANES)
    return pl.pallas_call(
        functools.partial(_attn_kernel, t=t, has_prefix=has_prefix, lam_init=lam_init),
        grid=(Bn, nq),
        in_specs=[pl.BlockSpec((t, W), lambda b, i: (b * nq + i, 0)),
                  pl.BlockSpec((S, W), lambda b, i: (b, 0)),
                  pl.BlockSpec((S, W), lambda b, i: (b, 0)),
                  pl.BlockSpec((META_ROWS, W), lambda b, i: (0, 0)),
                  pl.BlockSpec((META_ROWS, W), lambda b, i: (0, 0)),
                  pl.BlockSpec((ATTN_HEADS, 2, t, t), lambda b, i: (0, 0, 0, 0)),
                  pl.BlockSpec((ATTN_HEADS, 2, t, LANES), lambda b, i: (0, 0, 0, 0)),
                  pl.BlockSpec(memory_space=pltpu.SMEM),
                  pl.BlockSpec((1, ATTN_DV), lambda b, i: (0, 0))],
        out_specs=pl.BlockSpec((t, W), lambda b, i: (b * nq + i, 0)),
        out_shape=jax.ShapeDtypeStruct((Bn * S, W), BF16),
        scratch_shapes=[pltpu.VMEM((ATTN_HEADS, 2, nq, t, t), F32),
                        pltpu.VMEM(hm_shape, F32),
                        pltpu.VMEM(hm_shape, F32),
                        pltpu.VMEM(hm_shape, F32),
                        pltpu.VMEM(hm_shape, F32)],
        compiler_params=pltpu.CompilerParams(
            dimension_semantics=("parallel", "arbitrary"), vmem_limit_bytes=VMEM_LIMIT),
        name="diff_attn",
    )(q, k, v, kp, vp, dtab, ptab, lam, subln_g.reshape(1, -1))


def _lam_kernel(q1_ref, k1_ref, q2_ref, k2_ref, o_ref, *, lam_init):
    a = jnp.sum(q1_ref[...] * k1_ref[...], axis=-1, keepdims=True)
    b = jnp.sum(q2_ref[...] * k2_ref[...], axis=-1, keepdims=True)
    o_ref[...] = jnp.broadcast_to(jnp.exp(a) - jnp.exp(b) + lam_init, o_ref.shape)


def _lambda(q1, k1, q2, k2, lam_init):
    r = lambda x: x.reshape(1, ATTN_DK)
    return pl.pallas_call(
        functools.partial(_lam_kernel, lam_init=lam_init),
        out_shape=jax.ShapeDtypeStruct((1, LANES), F32),
        name="lambda",
    )(r(q1), r(k1), r(q2), r(k2))


def _outproj_kernel(a_ref, c_ref, h_ref, w_ref, g_ref, b_ref, o_ref):
    mix = jnp.dot(a_ref[...], w_ref[0:ATTN_WIDTH, :], preferred_element_type=F32)
    mix = mix + jnp.dot(c_ref[...], w_ref[ATTN_WIDTH:, :], preferred_element_type=F32)
    o_ref[...] = _ln(DEEPNORM_ALPHA * h_ref[...] + mix, g_ref[...], b_ref[...])


def _outproj_ln(a, c, h, w_out, g, b, tm):
    T = h.shape[0]
    row = lambda i: (i, 0)
    full = lambda i: (0, 0)
    return pl.pallas_call(
        _outproj_kernel,
        grid=(T // tm,),
        in_specs=[pl.BlockSpec((tm, ATTN_WIDTH), row),
                  pl.BlockSpec((tm, CONV_WIDTH), row),
                  pl.BlockSpec((tm, D_MODEL), row),
                  pl.BlockSpec((D_MODEL, D_MODEL), full),
                  pl.BlockSpec((1, D_MODEL), full),
                  pl.BlockSpec((1, D_MODEL), full)],
        out_specs=pl.BlockSpec((tm, D_MODEL), row),
        out_shape=jax.ShapeDtypeStruct((T, D_MODEL), F32),
        compiler_params=pltpu.CompilerParams(
            dimension_semantics=("parallel",), vmem_limit_bytes=VMEM_LIMIT),
        name="outproj_ln",
    )(a, c, h, w_out, g.reshape(1, -1), b.reshape(1, -1))


NOT_SELECTED = 64.0


def _top16_ordered(s):
    vals = []
    cur = s
    rank = jnp.full(s.shape, NOT_SELECTED, F32)
    rows = lax.broadcasted_iota(jnp.int32, s.shape, 0).astype(F32)
    for i in range(PEER_TOPK):
        m = jnp.max(cur, axis=0, keepdims=True)
        vals.append(m)
        first = jnp.min(jnp.where(cur == m, rows, float(PEER_NKEYS)), axis=0, keepdims=True)
        hit = rows == first
        rank = jnp.where(hit, float(i), rank)
        cur = jnp.where(hit, -jnp.inf, cur)
    return vals, rank


def _oddeven_merge_sort_pairs(n):
    pairs = []

    def merge(lo, hi, r):
        step = r * 2
        if step < hi - lo:
            merge(lo, hi, step)
            merge(lo + r, hi, step)
            pairs.extend((i, i + r) for i in range(lo + r, hi - r, step))
        else:
            pairs.append((lo, lo + r))

    def sort(lo, hi):
        if hi - lo >= 1:
            mid = lo + (hi - lo) // 2
            sort(lo, mid)
            sort(mid + 1, hi)
            merge(lo, hi, 1)

    sort(0, n - 1)
    return pairs


_SORT16 = _oddeven_merge_sort_pairs(PEER_TOPK)


def _leaders(s):
    x = [s[8 * i:8 * i + 8, :] for i in range(s.shape[0] // 8)]
    x = x + [jnp.full((8, LANES), -jnp.inf, F32)] * (PEER_TOPK - len(x))

    def exchange(x, i, j):
        x[i], x[j] = jnp.maximum(x[i], x[j]), jnp.minimum(x[i], x[j])

    for i, j in _SORT16:
        exchange(x, i, j)
    for shift in (4, 2, 1):
        y = [pltpu.roll(v, shift, axis=0) for v in x]
        x = [jnp.maximum(x[i], y[PEER_TOPK - 1 - i]) for i in range(PEER_TOPK)]
        for d in (8, 4, 2, 1):
            for i in range(PEER_TOPK):
                if i & d == 0:
                    exchange(x, i, i + d)
    return [v[0:1, :] for v in x]


def _dup_bf16(x):
    b = pltpu.bitcast(x, jnp.uint32)
    hi = (b + jnp.uint32(0x7FFF) + ((b >> 16) & jnp.uint32(1))) & jnp.uint32(0xFFFF0000)
    return hi | (hi >> 16)


def _pair_candidates(c1, c2):
    c1s = jnp.concatenate(c1, axis=0)
    c2s = jnp.concatenate(c2, axis=0)
    lo = c2s[0:8, :]
    row = lax.broadcasted_iota(jnp.int32, lo.shape, 0)
    parts = [c1s + c2[0], c1s[0:8, :] + c2[1], c1s[0:8, :] + c2[2], c1s[0:8, :] + c2[3],
             c1[0] + c2s[8:16, :]]
    for i in range(3):
        parts.append(jnp.where(row >= 4, c1[i] + lo, -jnp.inf))
    return jnp.concatenate(parts, axis=0), c2s


def _gate_tables(s1, s2, c1, c2, rank2, count, z):
    e1 = jnp.exp(s1 - c1[0]) * (0.5 / z)
    e2 = jnp.exp(s2 - c2[0])
    return rank2.astype(BF16), e2.astype(BF16), _dup_bf16(count), _dup_bf16(e1)


def _route_distinct(s1, s2, c1, c2, rank2):
    cand, c2s = _pair_candidates(c1, c2)
    tau = _leaders(cand)[PEER_TOPK - 1]
    removed = jnp.sum(jnp.where(cand >= tau, 1.0, 0.0), axis=0, keepdims=True)
    z = jnp.sum(jnp.where(cand >= tau, jnp.exp(cand - (c1[0] + c2[0])), 0.0), axis=0, keepdims=True)
    count = jnp.zeros(s1.shape, F32)
    for i in range(PEER_TOPK):
        n = jnp.sum(jnp.where(c1[i] + c2s >= tau, 1.0, 0.0), axis=0, keepdims=True)
        count = jnp.where(s1 == c1[i], n, count)
    return _gate_tables(s1, s2, c1, c2, rank2, count, z), removed


def _route_ties(s1, s2):
    c1, rank1 = _top16_ordered(s1)
    c2, rank2 = _top16_ordered(s2)
    cand, c2s = _pair_candidates(c1, c2)
    top = c1[0] + c2[0]
    cur = cand
    tau = top
    got = jnp.zeros_like(top)
    above = got
    for _ in range(PEER_TOPK):
        m = jnp.max(cur, axis=0, keepdims=True)
        hit = cur == m
        filling = got < float(PEER_TOPK)
        tau = jnp.where(filling, m, tau)
        above = jnp.where(filling, got, above)
        got = got + jnp.sum(jnp.where(hit, 1.0, 0.0), axis=0, keepdims=True)
        cur = jnp.where(hit, -jnp.inf, cur)
    at_tau = float(PEER_TOPK) - above
    z = jnp.sum(jnp.where(cand > tau, jnp.exp(cand - top), 0.0), axis=0, keepdims=True)
    z = z + at_tau * jnp.exp(tau - top)
    count = jnp.zeros(s1.shape, F32)
    used = jnp.zeros_like(tau)
    for i in range(PEER_TOPK):
        sums = c1[i] + c2s
        n_gt = jnp.sum(jnp.where(sums > tau, 1.0, 0.0), axis=0, keepdims=True)
        n_eq = jnp.sum(jnp.where(sums == tau, 1.0, 0.0), axis=0, keepdims=True)
        take = jnp.minimum(n_eq, jnp.maximum(at_tau - used, 0.0))
        used = used + take
        count = jnp.where(rank1 == float(i), n_gt + take, count)
    return _gate_tables(s1, s2, c1, c2, rank2, count, z)


def _select(s1, s2):
    c1 = _leaders(s1)
    c2 = _leaders(s2)
    rank2 = jnp.full(s2.shape, NOT_SELECTED, F32)
    for i in range(PEER_TOPK):
        rank2 = jnp.where(s2 == c2[i], float(i), rank2)
    k = float(PEER_TOPK)
    n1 = jnp.sum(jnp.where(s1 >= c1[PEER_TOPK - 1], 1.0, 0.0), axis=0, keepdims=True)
    n2 = jnp.sum(jnp.where(rank2 < NOT_SELECTED, 1.0, 0.0), axis=0, keepdims=True)
    dup = jnp.zeros_like(n1)
    for i in range(PEER_TOPK - 1):
        dup = dup + jnp.where(c1[i] == c1[i + 1], 1.0, 0.0) + jnp.where(c2[i] == c2[i + 1], 1.0, 0.0)
    tables, removed3 = _route_distinct(s1, s2, c1, c2, rank2)
    tied = jnp.max(jnp.abs(n1 - k) + jnp.abs(n2 - k) + dup + jnp.abs(removed3 - k)) > 0.0
    return tables, tied


def _peer_kernel(h_ref, wq_ref, sk_ref, u_ref, vt_ref, g_ref, b_ref, o_ref,
                 hb_ref, qp_ref, s_ref, r2_ref, e2_ref, n1_ref, e1_ref, tied_ref, act_ref, wact_ref, acc_ref,
                 *, tT, EB):
    j = pl.program_id(1)
    nl = tT // LANES
    na = EB // PEER_NKEYS
    RB = 16
    AG = 8
    nr = PEER_NKEYS // RB

    @pl.when(j == 0)
    def _():
        hb = h_ref[...].astype(BF16)
        hb_ref[...] = hb
        QC = 512
        for c in range(PEER_HEADS * PEER_DQ // QC):
            qp_ref[c * QC:(c + 1) * QC, :] = lax.dot_general(
                wq_ref[c * QC:(c + 1) * QC, :], hb, _NT, preferred_element_type=F32).astype(BF16)
        half = PEER_DQ // 2
        for hp in range(2 * PEER_HEADS):
            sT = jnp.dot(sk_ref[hp], qp_ref[hp * half:(hp + 1) * half, :], preferred_element_type=F32)
            for l in range(nl):
                s_ref[hp, l] = sT[:, l * LANES:(l + 1) * LANES]

        def select(idx, carry):
            h = idx // (nl // lg)
            l0 = (idx % (nl // lg)) * lg
            for dl in range(lg):
                l = l0 + dl
                tables, tied = _select(s_ref[2 * h, l], s_ref[2 * h + 1, l])
                r2_ref[h, l], e2_ref[h, l], n1_ref[h, l], e1_ref[h, l] = tables
                tied_ref[h * nl + l] = tied.astype(jnp.int32)
            return carry

        lg = 2 if nl % 2 == 0 else 1
        lax.fori_loop(0, PEER_HEADS * nl // lg, select, 0)

        def redo_ties(idx, carry):
            @pl.when(tied_ref[idx] != 0)
            def _():
                h = idx // nl
                l = idx % nl
                r2_ref[h, l], e2_ref[h, l], n1_ref[h, l], e1_ref[h, l] = _route_ties(s_ref[2 * h, l], s_ref[2 * h + 1, l])
            return carry

        lax.fori_loop(0, PEER_HEADS * nl, redo_ties, 0)
        acc_ref[...] = jnp.zeros_like(acc_ref)

    act_ref[...] = lax.dot_general(u_ref[...], hb_ref[...], _NT, preferred_element_type=F32)

    def row_bf16(ref, h, l, a):
        return pltpu.bitcast(jnp.broadcast_to(ref[h, l, pl.ds(a, 1), :], (8, LANES)), BF16)

    zero = jnp.zeros((), BF16)

    def gate_rows(t, carry):
        for l in range(nl):
            lanes = slice(l * LANES, (l + 1) * LANES)
            for aa in range(AG):
                al = t * AG + aa
                a = j * na + al
                w = [jnp.zeros((RB, LANES), BF16)] * nr
                for h in range(PEER_HEADS):
                    n1r = row_bf16(n1_ref, h, l, a)
                    e1r = row_bf16(e1_ref, h, l, a)
                    for r in range(nr):
                        r2c = r2_ref[h, l, r * RB:(r + 1) * RB, :]
                        e2c = e2_ref[h, l, r * RB:(r + 1) * RB, :]
                        w[r] = w[r] + jnp.minimum(e1r * e2c, jnp.maximum(n1r - r2c, zero))
                for r in range(nr):
                    rows = pl.ds(pl.multiple_of(al * PEER_NKEYS + r * RB, RB), RB)
                    x = act_ref[rows, lanes]
                    gelu2 = x * (1.0 + lax.erf(x * SQRT_HALF))
                    wact_ref[rows, lanes] = w[r] * gelu2.astype(BF16)
        return carry

    lax.fori_loop(0, na // AG, gate_rows, 0)
    acc_ref[...] += jnp.dot(vt_ref[...], wact_ref[...], preferred_element_type=F32)

    @pl.when(j == pl.num_programs(1) - 1)
    def _():
        y = DEEPNORM_ALPHA * h_ref[...] + acc_ref[...].T
        o_ref[...] = _ln(y, g_ref[...], b_ref[...])


def _peer_ln(h, wq_t, sk, u, v_t, g, b, tT, EB):
    T = h.shape[0]
    nl = tT // LANES
    row = lambda i, j: (i, 0)
    full2 = lambda i, j: (0, 0)
    head_shape = (PEER_HEADS, nl, PEER_NKEYS, LANES)
    return pl.pallas_call(
        functools.partial(_peer_kernel, tT=tT, EB=EB),
        grid=(T // tT, PEER_N // EB),
        in_specs=[pl.BlockSpec((tT, D_MODEL), row),
                  pl.BlockSpec((PEER_HEADS * PEER_DQ, D_MODEL), full2, pipeline_mode=pl.Buffered(1)),
                  pl.BlockSpec((2 * PEER_HEADS, PEER_NKEYS, PEER_DQ // 2), lambda i, j: (0, 0, 0)),
                  pl.BlockSpec((EB, D_MODEL), lambda i, j: (j, 0)),
                  pl.BlockSpec((D_MODEL, EB), lambda i, j: (0, j)),
                  pl.BlockSpec((1, D_MODEL), full2),
                  pl.BlockSpec((1, D_MODEL), full2)],
        out_specs=pl.BlockSpec((tT, D_MODEL), row),
        out_shape=jax.ShapeDtypeStruct((T, D_MODEL), F32),
        scratch_shapes=[pltpu.VMEM((tT, D_MODEL), BF16),
                        pltpu.VMEM((PEER_HEADS * PEER_DQ, tT), BF16),
                        pltpu.VMEM((2 * PEER_HEADS, nl, PEER_NKEYS, LANES), F32),
                        pltpu.VMEM(head_shape, BF16),
                        pltpu.VMEM(head_shape, BF16),
                        pltpu.VMEM(head_shape, jnp.uint32),
                        pltpu.VMEM(head_shape, jnp.uint32),
                        pltpu.SMEM((PEER_HEADS * nl,), jnp.int32),
                        pltpu.VMEM((EB, tT), F32),
                        pltpu.VMEM((EB, tT), BF16),
                        pltpu.VMEM((D_MODEL, tT), F32)],
        compiler_params=pltpu.CompilerParams(
            dimension_semantics=("parallel", "arbitrary"), vmem_limit_bytes=VMEM_LIMIT),
        name="peer_ln",
    )(h, wq_t, sk, u, v_t, g.reshape(1, -1), b.reshape(1, -1))


def _tile(n, pref):
    return pref if n % pref == 0 else n


def _run_layer(l, h, Bn, S, prefix, params, tabs, lam, last_meta=False):
    (w_in, conv_w, subln_g, w_out, ln1_g, ln1_b, wq_t, sk, u, v_t, ln2_g, ln2_b) = params
    lam_init = 0.8 - 0.6 * math.exp(-0.3 * l)
    is_meta = prefix is None
    t = _tile(S, 256)
    ts = _tile(S, 512)
    halo = jnp.zeros((8, CONV_WIDTH), F32) if is_meta else prefix[2]
    outs = _proj_conv(h, w_in, conv_w, halo, Bn, S, ts, emit_u=is_meta)
    q, k, v, c = outs[:4]
    new_prefix = (k, v, outs[4][N_META - 8:N_META]) if is_meta else None
    if last_meta:
        return None, new_prefix
    kp, vp = (k, v) if is_meta else prefix[:2]
    a = _diff_attention(q, k, v, kp, vp, tabs[0], tabs[1], lam, subln_g, Bn, S, t,
                        has_prefix=not is_meta, lam_init=lam_init)
    T = Bn * S
    h1 = _outproj_ln(a, c, h, w_out, ln1_g, ln1_b, _tile(T, 512))
    h2 = _peer_ln(h1, wq_t, sk, u, v_t, ln2_g, ln2_b, _tile(T, 512), PEER_BLOCK)
    return h2, new_prefix


def kernel(x, meta_tokens, ln_in_g, ln_in_b, rel_bias, w_in, conv_w, lambda_q1, lambda_k1, lambda_q2,
           lambda_k2, subln_g, w_out, ln1_g, ln1_b, peer_w_q, peer_sub_keys, peer_u, peer_v, ln2_g, ln2_b):
    Bn, S, _ = x.shape
    depth = w_in.shape[0]
    w_in_b = w_in.astype(BF16)
    w_out_b = w_out.astype(BF16)
    wq_t = jnp.swapaxes(peer_w_q, 1, 2).astype(BF16)
    sk = peer_sub_keys.reshape(depth, 2 * PEER_HEADS, PEER_NKEYS, PEER_DQ // 2).astype(BF16)
    u_b = peer_u.astype(BF16)
    v_t = jnp.swapaxes(peer_v, 1, 2).astype(BF16)

    t_main = _tile(S, 256)
    tabs_main = _bias_tables(rel_bias, t_main)
    tabs_meta = _bias_tables(rel_bias, META_ROWS)

    meta = jnp.zeros((META_ROWS, D_MODEL), F32).at[:N_META].set(meta_tokens.astype(F32))
    hm = _layer_norm_rows(meta, ln_in_g, ln_in_b, META_ROWS)
    h = _layer_norm_rows(x.reshape(Bn * S, D_MODEL), ln_in_g, ln_in_b, _tile(Bn * S, 512))

    for l in range(depth):
        lam_init = 0.8 - 0.6 * math.exp(-0.3 * l)
        lam = _lambda(lambda_q1[l], lambda_k1[l], lambda_q2[l], lambda_k2[l], lam_init)
        params = (w_in_b[l], conv_w[l], subln_g[l], w_out_b[l], ln1_g[l], ln1_b[l], wq_t[l], sk[l],
                  u_b[l], v_t[l], ln2_g[l], ln2_b[l])
        hm, prefix = _run_layer(l, hm, 1, META_ROWS, None, params, tabs_meta, lam,
                                last_meta=(l == depth - 1))
        h, _ = _run_layer(l, h, Bn, S, prefix, params, tabs_main, lam)
    return h.reshape(Bn, S, D_MODEL)
```

```python
import functools
import math

import jax
import jax.numpy as jnp
from jax import lax
from jax.experimental import pallas as pl
from jax.experimental.pallas import tpu as pltpu

F32 = jnp.float32
BF16 = jnp.bfloat16

D_MODEL = 1024
DEPTH = 4
N_META = 16
META_ROWS = 128
ATTN_HEADS = 4
ATTN_DK = 64
ATTN_DV = 2 * ATTN_DK
ATTN_WIDTH = ATTN_HEADS * ATTN_DV
CONV_WIDTH = D_MODEL - ATTN_WIDTH
CONV_K = 3
IN_COLS = 3 * ATTN_WIDTH + 3 * CONV_WIDTH
REL_BUCKETS = 32
REL_MAX_DIST = 128
PEER_HEADS = 8
PEER_NKEYS = 128
PEER_N = PEER_NKEYS * PEER_NKEYS
PEER_DQ = 256
PEER_TOPK = 16
DEEPNORM_ALPHA = (2 * DEPTH) ** 0.25
LN_EPS = 1e-5
RMS_EPS = 1e-5
NEG_BIG = -1e30
SQRT_HALF = math.sqrt(0.5)

PEER_BLOCK = 2048
LANES = 128
VMEM_LIMIT = 60000 * 1024

_NT = (((1,), (1,)), ((), ()))


def _ln(y, g, b):
    mu = jnp.mean(y, axis=-1, keepdims=True)
    yc = y - mu
    var = jnp.mean(yc * yc, axis=-1, keepdims=True)
    return yc * lax.rsqrt(var + LN_EPS) * g + b


def _ln_kernel(x_ref, g_ref, b_ref, o_ref):
    o_ref[...] = _ln(x_ref[...], g_ref[...], b_ref[...])


def _layer_norm_rows(x, g, b, tm):
    T = x.shape[0]
    return pl.pallas_call(
        _ln_kernel,
        grid=(T // tm,),
        in_specs=[pl.BlockSpec((tm, D_MODEL), lambda i: (i, 0)),
                  pl.BlockSpec((1, D_MODEL), lambda i: (0, 0)),
                  pl.BlockSpec((1, D_MODEL), lambda i: (0, 0))],
        out_specs=pl.BlockSpec((tm, D_MODEL), lambda i: (i, 0)),
        out_shape=jax.ShapeDtypeStruct((T, D_MODEL), F32),
        compiler_params=pltpu.CompilerParams(dimension_semantics=("parallel",)),
        name="ln_in",
    )(x, g.reshape(1, -1), b.reshape(1, -1))


def _proj_kernel(h_ref, w_ref, cw_ref, halo_ref, q_ref, k_ref, v_ref, c_ref, *rest, ts, tk, emit_u):
    if emit_u:
        u_ref, ubuf = rest
    else:
        (ubuf,) = rest
    W = ATTN_WIDTH

    @pl.when(pl.program_id(1) == 0)
    def _():
        ubuf[0:8, :] = halo_ref[...]

    hb = h_ref[...].astype(BF16)

    def col(c):
        return jnp.dot(hb, w_ref[:, c * W:(c + 1) * W], preferred_element_type=F32)

    q_ref[...] = (col(0) * (ATTN_DK ** -0.5)).astype(BF16)
    kt = col(1).T.astype(BF16)
    for c in range(ts // tk):
        k_ref[c] = kt[:, c * tk:(c + 1) * tk]
    v_ref[...] = col(2).astype(BF16)
    u = col(4) * col(5)
    ubuf[8:8 + ts, :] = u
    if emit_u:
        u_ref[...] = u
    conv = (cw_ref[0:1, :] * ubuf[6:6 + ts, :] + cw_ref[1:2, :] * ubuf[7:7 + ts, :]
            + cw_ref[2:3, :] * u)
    c_ref[...] = (col(3) * conv).astype(BF16)
    ubuf[0:8, :] = ubuf[ts:ts + 8, :]


def _proj_conv(h, w_in, conv_w, halo, Bn, S, ts, tk, emit_u):
    T = Bn * S
    nS = S // ts
    row = lambda b, s: (b * nS + s, 0)
    full = lambda b, s: (0, 0)
    out_shape = [jax.ShapeDtypeStruct((T, ATTN_WIDTH), BF16)] * 4
    out_specs = [pl.BlockSpec((ts, ATTN_WIDTH), row)] * 4
    out_shape[1] = jax.ShapeDtypeStruct((T // tk, ATTN_WIDTH, tk), BF16)
    out_specs[1] = pl.BlockSpec((ts // tk, ATTN_WIDTH, tk), lambda b, s: (b * nS + s, 0, 0))
    if emit_u:
        out_shape = out_shape + [jax.ShapeDtypeStruct((T, CONV_WIDTH), F32)]
        out_specs = out_specs + [pl.BlockSpec((ts, CONV_WIDTH), row)]
    return pl.pallas_call(
        functools.partial(_proj_kernel, ts=ts, tk=tk, emit_u=emit_u),
        grid=(Bn, nS),
        in_specs=[pl.BlockSpec((ts, D_MODEL), row),
                  pl.BlockSpec((D_MODEL, IN_COLS), full),
                  pl.BlockSpec((CONV_K, CONV_WIDTH), full),
                  pl.BlockSpec((8, CONV_WIDTH), full)],
        out_specs=out_specs,
        out_shape=out_shape,
        scratch_shapes=[pltpu.VMEM((ts + 8, CONV_WIDTH), F32)],
        compiler_params=pltpu.CompilerParams(
            dimension_semantics=("parallel", "arbitrary"), vmem_limit_bytes=VMEM_LIMIT),
        name="proj_conv",
    )(h, w_in, conv_w, halo)


def _bias_kernel(rb_ref, d_ref, p_ref, *, t):
    h = pl.program_id(0)
    far = rb_ref[REL_BUCKETS - 1, h]

    def table(n):
        max_exact = REL_BUCKETS // 2
        nf = jnp.maximum(n, 1).astype(F32)
        large = max_exact + (jnp.log(nf / max_exact) / math.log(REL_MAX_DIST / max_exact)
                             * (REL_BUCKETS - max_exact)).astype(jnp.int32)
        large = jnp.minimum(large, REL_BUCKETS - 1)
        bucket = jnp.where(n < max_exact, n, large)
        out = jnp.zeros(n.shape, F32)
        for bkt in range(REL_BUCKETS):
            out = jnp.where(bucket == bkt, rb_ref[bkt, h], out)
        return out - far

    r = lax.broadcasted_iota(jnp.int32, (t, t), 0)
    c = lax.broadcasted_iota(jnp.int32, (t, t), 1)
    d_ref[0, 0] = jnp.where(c <= r, table(jnp.maximum(r - c, 0)), NEG_BIG)
    d_ref[0, 1] = table(t + r - c)
    rp = lax.broadcasted_iota(jnp.int32, (t, LANES), 0)
    cp = lax.broadcasted_iota(jnp.int32, (t, LANES), 1)
    p_ref[0, 0] = jnp.where(cp < N_META, table(N_META + rp - cp), NEG_BIG)
    p_ref[0, 1] = jnp.where(cp < N_META, 0.0, NEG_BIG)


def _bias_tables(rel_bias, t):
    return pl.pallas_call(
        functools.partial(_bias_kernel, t=t),
        grid=(ATTN_HEADS,),
        in_specs=[pl.BlockSpec(memory_space=pltpu.SMEM)],
        out_specs=[pl.BlockSpec((1, 2, t, t), lambda h: (h, 0, 0, 0)),
                   pl.BlockSpec((1, 2, t, LANES), lambda h: (h, 0, 0, 0))],
        out_shape=[jax.ShapeDtypeStruct((ATTN_HEADS, 2, t, t), F32),
                   jax.ShapeDtypeStruct((ATTN_HEADS, 2, t, LANES), F32)],
        name="bias_tables",
    )(rel_bias)


def _attn_kernel(q_ref, k_ref, v_ref, kp_ref, vp_ref, d_ref, p_ref, lam_ref, g_ref, o_ref,
                 s_ref, sp_ref, m_ref, l_ref, acc_ref, *, t, has_prefix, lam_init):
    i = pl.program_id(1)
    nc = t // LANES
    hm = [(h, m) for h in range(ATTN_HEADS) for m in range(2)]

    def cols(h, m):
        c0 = h * ATTN_DV + m * ATTN_DK
        return slice(c0, c0 + ATTN_DK)

    def vcols(h):
        return slice(h * ATTN_DV, (h + 1) * ATTN_DV)

    def lane_fold(x, op):
        out = x[:, :LANES]
        for c in range(1, x.shape[1] // LANES):
            out = op(out, x[:, c * LANES:(c + 1) * LANES])
        return out

    def logits(h, m, kt_ref, j):
        return jnp.dot(q_ref[:, cols(h, m)], kt_ref[j, cols(h, m), :], preferred_element_type=F32)

    if has_prefix:
        first = jnp.minimum(i, 1)
        for h, m in hm:
            sp = logits(h, m, kp_ref, 0) + p_ref[h, first]
            sp_ref[h, m] = sp
            m_ref[h, m] = sp
    else:
        m_ref[...] = jnp.full(m_ref.shape, -jnp.inf, F32)

    def score_tile(j, near):
        for h, m in hm:
            s = logits(h, m, k_ref, j)
            if near:
                s = s + d_ref[h, i - j]
            s_ref[h, m, j] = s
            m_ref[h, m] = jnp.maximum(m_ref[h, m], lane_fold(s, jnp.maximum))

    def far_body(j, c):
        score_tile(j, False)
        return c

    def near_body(j, c):
        score_tile(j, True)
        return c

    lax.fori_loop(0, jnp.maximum(i - 1, 0), far_body, 0)
    lax.fori_loop(jnp.maximum(i - 1, 0), i + 1, near_body, 0)

    for h, m in hm:
        m_ref[h, m] = jnp.broadcast_to(jnp.max(m_ref[h, m], axis=-1, keepdims=True), (t, LANES))

    if has_prefix:
        for h, m in hm:
            p = jnp.exp(sp_ref[h, m] - m_ref[h, m])
            l_ref[h, m] = p
            acc_ref[h, m] = jnp.dot(p.astype(BF16), vp_ref[:, vcols(h)], preferred_element_type=F32)
    else:
        l_ref[...] = jnp.zeros_like(l_ref)
        acc_ref[...] = jnp.zeros_like(acc_ref)

    def value_body(j, c):
        rows = pl.ds(pl.multiple_of(j * t, t), t)
        for h, m in hm:
            p = jnp.exp(s_ref[h, m, j] - jnp.concatenate([m_ref[h, m]] * nc, axis=1))
            l_ref[h, m] += lane_fold(p, jnp.add)
            acc_ref[h, m] += jnp.dot(p.astype(BF16), v_ref[rows, vcols(h)], preferred_element_type=F32)
        return c

    lax.fori_loop(0, i + 1, value_body, 0)

    lam = lam_ref[0, 0]
    for h in range(ATTN_HEADS):
        o = [acc_ref[h, m] / jnp.sum(l_ref[h, m], axis=-1, keepdims=True) for m in range(2)]
        out = o[0] - lam * o[1]
        out = out * lax.rsqrt(jnp.mean(out * out, axis=-1, keepdims=True) + RMS_EPS)
        o_ref[:, vcols(h)] = (out * g_ref[...] * (1.0 - lam_init)).astype(BF16)


def _diff_attention(q, k, v, kp, vp, dtab, ptab, lam, subln_g, Bn, S, t, has_prefix, lam_init):
    nq = S // t
    W = ATTN_WIDTH
    hm_shape = (ATTN_HEADS, 2, t, LANES)
    return pl.pallas_call(
        functools.partial(_attn_kernel, t=t, has_prefix=has_prefix, lam_init=lam_init),
        grid=(Bn, nq),
        in_specs=[pl.BlockSpec((t, W), lambda b, i: (b * nq + i, 0)),
                  pl.BlockSpec((nq, W, t), lambda b, i: (b, 0, 0)),
                  pl.BlockSpec((S, W), lambda b, i: (b, 0)),
                  pl.BlockSpec((1, W, META_ROWS), lambda b, i: (0, 0, 0)),
                  pl.BlockSpec((META_ROWS, W), lambda b, i: (0, 0)),
                  pl.BlockSpec((ATTN_HEADS, 2, t, t), lambda b, i: (0, 0, 0, 0)),
                  pl.BlockSpec((ATTN_HEADS, 2, t, LANES), lambda b, i: (0, 0, 0, 0)),
                  pl.BlockSpec(memory_space=pltpu.SMEM),
                  pl.BlockSpec((1, ATTN_DV), lambda b, i: (0, 0))],
        out_specs=pl.BlockSpec((t, W), lambda b, i: (b * nq + i, 0)),
        out_shape=jax.ShapeDtypeStruct((Bn * S, W), BF16),
        scratch_shapes=[pltpu.VMEM((ATTN_HEADS, 2, nq, t, t), F32),
                        pltpu.VMEM(hm_shape, F32),
                        pltpu.VMEM(hm_shape, F32),
                        pltpu.VMEM(hm_shape, F32),
                        pltpu.VMEM(hm_shape, F32)],
        compiler_params=pltpu.CompilerParams(
            dimension_semantics=("parallel", "arbitrary"), vmem_limit_bytes=VMEM_LIMIT),
        name="diff_attn",
    )(q, k, v, kp, vp, dtab, ptab, lam, subln_g.reshape(1, -1))


def _lam_kernel(q1_ref, k1_ref, q2_ref, k2_ref, o_ref, *, lam_init):
    a = jnp.sum(q1_ref[...] * k1_ref[...], axis=-1, keepdims=True)
    b = jnp.sum(q2_ref[...] * k2_ref[...], axis=-1, keepdims=True)
    o_ref[...] = jnp.broadcast_to(jnp.exp(a) - jnp.exp(b) + lam_init, o_ref.shape)


def _lambda(q1, k1, q2, k2, lam_init):
    r = lambda x: x.reshape(1, ATTN_DK)
    return pl.pallas_call(
        functools.partial(_lam_kernel, lam_init=lam_init),
        out_shape=jax.ShapeDtypeStruct((1, LANES), F32),
        name="lambda",
    )(r(q1), r(k1), r(q2), r(k2))


def _outproj_kernel(a_ref, c_ref, h_ref, w_ref, g_ref, b_ref, o_ref):
    mix = jnp.dot(a_ref[...], w_ref[0:ATTN_WIDTH, :], preferred_element_type=F32)
    mix = mix + jnp.dot(c_ref[...], w_ref[ATTN_WIDTH:, :], preferred_element_type=F32)
    o_ref[...] = _ln(DEEPNORM_ALPHA * h_ref[...] + mix, g_ref[...], b_ref[...])


def _outproj_ln(a, c, h, w_out, g, b, tm):
    T = h.shape[0]
    row = lambda i: (i, 0)
    full = lambda i: (0, 0)
    return pl.pallas_call(
        _outproj_kernel,
        grid=(T // tm,),
        in_specs=[pl.BlockSpec((tm, ATTN_WIDTH), row),
                  pl.BlockSpec((tm, CONV_WIDTH), row),
                  pl.BlockSpec((tm, D_MODEL), row),
                  pl.BlockSpec((D_MODEL, D_MODEL), full),
                  pl.BlockSpec((1, D_MODEL), full),
                  pl.BlockSpec((1, D_MODEL), full)],
        out_specs=pl.BlockSpec((tm, D_MODEL), row),
        out_shape=jax.ShapeDtypeStruct((T, D_MODEL), F32),
        compiler_params=pltpu.CompilerParams(
            dimension_semantics=("parallel",), vmem_limit_bytes=VMEM_LIMIT),
        name="outproj_ln",
    )(a, c, h, w_out, g.reshape(1, -1), b.reshape(1, -1))


NOT_SELECTED = 64.0


def _top16_ordered(s):
    vals = []
    cur = s
    rank = jnp.full(s.shape, NOT_SELECTED, F32)
    rows = lax.broadcasted_iota(jnp.int32, s.shape, 0).astype(F32)
    for i in range(PEER_TOPK):
        m = jnp.max(cur, axis=0, keepdims=True)
        vals.append(m)
        first = jnp.min(jnp.where(cur == m, rows, float(PEER_NKEYS)), axis=0, keepdims=True)
        hit = rows == first
        rank = jnp.where(hit, float(i), rank)
        cur = jnp.where(hit, -jnp.inf, cur)
    return vals, rank


def _oddeven_merge_sort_pairs(n):
    pairs = []

    def merge(lo, hi, r):
        step = r * 2
        if step < hi - lo:
            merge(lo, hi, step)
            merge(lo + r, hi, step)
            pairs.extend((i, i + r) for i in range(lo + r, hi - r, step))
        else:
            pairs.append((lo, lo + r))

    def sort(lo, hi):
        if hi - lo >= 1:
            mid = lo + (hi - lo) // 2
            sort(lo, mid)
            sort(mid + 1, hi)
            merge(lo, hi, 1)

    sort(0, n - 1)
    return pairs


_SORT16 = _oddeven_merge_sort_pairs(PEER_TOPK)


def _leaders(s):
    x = [s[8 * i:8 * i + 8, :] for i in range(s.shape[0] // 8)]
    x = x + [jnp.full((8, LANES), -jnp.inf, F32)] * (PEER_TOPK - len(x))

    def exchange(x, i, j):
        x[i], x[j] = jnp.maximum(x[i], x[j]), jnp.minimum(x[i], x[j])

    for i, j in _SORT16:
        exchange(x, i, j)
    for shift in (4, 2, 1):
        y = [pltpu.roll(v, shift, axis=0) for v in x]
        x = [jnp.maximum(x[i], y[PEER_TOPK - 1 - i]) for i in range(PEER_TOPK)]
        for d in (8, 4, 2, 1):
            for i in range(PEER_TOPK):
                if i & d == 0:
                    exchange(x, i, i + d)
    return [v[0:1, :] for v in x]


def _dup_bf16(x):
    b = pltpu.bitcast(x, jnp.uint32)
    hi = (b + jnp.uint32(0x7FFF) + ((b >> 16) & jnp.uint32(1))) & jnp.uint32(0xFFFF0000)
    return hi | (hi >> 16)


def _pair_candidates(c1, c2):
    c1s = jnp.concatenate(c1, axis=0)
    c2s = jnp.concatenate(c2, axis=0)
    lo = c2s[0:8, :]
    row = lax.broadcasted_iota(jnp.int32, lo.shape, 0)
    parts = [c1s + c2[0], c1s[0:8, :] + c2[1], c1s[0:8, :] + c2[2], c1s[0:8, :] + c2[3],
             c1[0] + c2s[8:16, :]]
    for i in range(3):
        parts.append(jnp.where(row >= 4, c1[i] + lo, -jnp.inf))
    return jnp.concatenate(parts, axis=0), c2s


def _gate_tables(s1, s2, c1, c2, rank2, count, z):
    e1 = jnp.exp(s1 - c1[0]) * (0.5 / z)
    e2 = jnp.exp(s2 - c2[0])
    return rank2.astype(BF16), e2.astype(BF16), _dup_bf16(count), _dup_bf16(e1)


def _route_distinct(s1, s2, c1, c2, rank2):
    cand, c2s = _pair_candidates(c1, c2)
    tau = _leaders(cand)[PEER_TOPK - 1]
    removed = jnp.sum(jnp.where(cand >= tau, 1.0, 0.0), axis=0, keepdims=True)
    z = jnp.sum(jnp.where(cand >= tau, jnp.exp(cand - (c1[0] + c2[0])), 0.0), axis=0, keepdims=True)
    count = jnp.zeros(s1.shape, F32)
    for i in range(PEER_TOPK):
        n = jnp.sum(jnp.where(c1[i] + c2s >= tau, 1.0, 0.0), axis=0, keepdims=True)
        count = jnp.where(s1 == c1[i], n, count)
    return _gate_tables(s1, s2, c1, c2, rank2, count, z), removed


def _route_ties(s1, s2):
    c1, rank1 = _top16_ordered(s1)
    c2, rank2 = _top16_ordered(s2)
    cand, c2s = _pair_candidates(c1, c2)
    top = c1[0] + c2[0]
    cur = cand
    tau = top
    got = jnp.zeros_like(top)
    above = got
    for _ in range(PEER_TOPK):
        m = jnp.max(cur, axis=0, keepdims=True)
        hit = cur == m
        filling = got < float(PEER_TOPK)
        tau = jnp.where(filling, m, tau)
        above = jnp.where(filling, got, above)
        got = got + jnp.sum(jnp.where(hit, 1.0, 0.0), axis=0, keepdims=True)
        cur = jnp.where(hit, -jnp.inf, cur)
    at_tau = float(PEER_TOPK) - above
    z = jnp.sum(jnp.where(cand > tau, jnp.exp(cand - top), 0.0), axis=0, keepdims=True)
    z = z + at_tau * jnp.exp(tau - top)
    count = jnp.zeros(s1.shape, F32)
    used = jnp.zeros_like(tau)
    for i in range(PEER_TOPK):
        sums = c1[i] + c2s
        n_gt = jnp.sum(jnp.where(sums > tau, 1.0, 0.0), axis=0, keepdims=True)
        n_eq = jnp.sum(jnp.where(sums == tau, 1.0, 0.0), axis=0, keepdims=True)
        take = jnp.minimum(n_eq, jnp.maximum(at_tau - used, 0.0))
        used = used + take
        count = jnp.where(rank1 == float(i), n_gt + take, count)
    return _gate_tables(s1, s2, c1, c2, rank2, count, z)


def _select(s1, s2):
    c1 = _leaders(s1)
    c2 = _leaders(s2)
    rank2 = jnp.full(s2.shape, NOT_SELECTED, F32)
    for i in range(PEER_TOPK):
        rank2 = jnp.where(s2 == c2[i], float(i), rank2)
    k = float(PEER_TOPK)
    n1 = jnp.sum(jnp.where(s1 >= c1[PEER_TOPK - 1], 1.0, 0.0), axis=0, keepdims=True)
    n2 = jnp.sum(jnp.where(rank2 < NOT_SELECTED, 1.0, 0.0), axis=0, keepdims=True)
    dup = jnp.zeros_like(n1)
    for i in range(PEER_TOPK - 1):
        dup = dup + jnp.where(c1[i] == c1[i + 1], 1.0, 0.0) + jnp.where(c2[i] == c2[i + 1], 1.0, 0.0)
    tables, removed3 = _route_distinct(s1, s2, c1, c2, rank2)
    tied = jnp.max(jnp.abs(n1 - k) + jnp.abs(n2 - k) + dup + jnp.abs(removed3 - k)) > 0.0
    return tables, tied


def _peer_kernel(h_ref, wq_ref, sk_ref, u_ref, vt_ref, g_ref, b_ref, o_ref,
                 hb_ref, qp_ref, s_ref, r2_ref, e2_ref, n1_ref, e1_ref, tied_ref, act_ref, wact_ref, acc_ref,
                 *, tT, EB):
    j = pl.program_id(1)
    nl = tT // LANES
    na = EB // PEER_NKEYS
    RB = 16
    AG = 8
    nr = PEER_NKEYS // RB

    @pl.when(j == 0)
    def _():
        hb = h_ref[...].astype(BF16)
        hb_ref[...] = hb
        QC = 512
        for c in range(PEER_HEADS * PEER_DQ // QC):
            qp_ref[c * QC:(c + 1) * QC, :] = lax.dot_general(
                wq_ref[c * QC:(c + 1) * QC, :], hb, _NT, preferred_element_type=F32).astype(BF16)
        half = PEER_DQ // 2
        for hp in range(2 * PEER_HEADS):
            sT = jnp.dot(sk_ref[hp], qp_ref[hp * half:(hp + 1) * half, :], preferred_element_type=F32)
            for l in range(nl):
                s_ref[hp, l] = sT[:, l * LANES:(l + 1) * LANES]

        def select(idx, carry):
            h = idx // (nl // lg)
            l0 = (idx % (nl // lg)) * lg
            for dl in range(lg):
                l = l0 + dl
                tables, tied = _select(s_ref[2 * h, l], s_ref[2 * h + 1, l])
                r2_ref[h, l], e2_ref[h, l], n1_ref[h, l], e1_ref[h, l] = tables
                tied_ref[h * nl + l] = tied.astype(jnp.int32)
            return carry

        lg = 2 if nl % 2 == 0 else 1
        lax.fori_loop(0, PEER_HEADS * nl // lg, select, 0)

        def redo_ties(idx, carry):
            @pl.when(tied_ref[idx] != 0)
            def _():
                h = idx // nl
                l = idx % nl
                r2_ref[h, l], e2_ref[h, l], n1_ref[h, l], e1_ref[h, l] = _route_ties(s_ref[2 * h, l], s_ref[2 * h + 1, l])
            return carry

        lax.fori_loop(0, PEER_HEADS * nl, redo_ties, 0)
        acc_ref[...] = jnp.zeros_like(acc_ref)

    act_ref[...] = lax.dot_general(u_ref[...], hb_ref[...], _NT, preferred_element_type=F32)

    def row_bf16(ref, h, l, a):
        return pltpu.bitcast(jnp.broadcast_to(ref[h, l, pl.ds(a, 1), :], (8, LANES)), BF16)

    zero = jnp.zeros((), BF16)

    def gate_rows(t, carry):
        for l in range(nl):
            lanes = slice(l * LANES, (l + 1) * LANES)
            for aa in range(AG):
                al = t * AG + aa
                a = j * na + al
                w = [jnp.zeros((RB, LANES), BF16)] * nr
                for h in range(PEER_HEADS):
                    n1r = row_bf16(n1_ref, h, l, a)
                    e1r = row_bf16(e1_ref, h, l, a)
                    for r in range(nr):
                        r2c = r2_ref[h, l, r * RB:(r + 1) * RB, :]
                        e2c = e2_ref[h, l, r * RB:(r + 1) * RB, :]
                        w[r] = w[r] + jnp.minimum(e1r * e2c, jnp.maximum(n1r - r2c, zero))
                for r in range(nr):
                    rows = pl.ds(pl.multiple_of(al * PEER_NKEYS + r * RB, RB), RB)
                    x = act_ref[rows, lanes]
                    gelu2 = x * (1.0 + lax.erf(x * SQRT_HALF))
                    wact_ref[rows, lanes] = w[r] * gelu2.astype(BF16)
        return carry

    lax.fori_loop(0, na // AG, gate_rows, 0)
    acc_ref[...] += jnp.dot(vt_ref[...], wact_ref[...], preferred_element_type=F32)

    @pl.when(j == pl.num_programs(1) - 1)
    def _():
        y = DEEPNORM_ALPHA * h_ref[...] + acc_ref[...].T
        o_ref[...] = _ln(y, g_ref[...], b_ref[...])


def _peer_ln(h, wq_t, sk, u, v_t, g, b, tT, EB):
    T = h.shape[0]
    nl = tT // LANES
    row = lambda i, j: (i, 0)
    full2 = lambda i, j: (0, 0)
    head_shape = (PEER_HEADS, nl, PEER_NKEYS, LANES)
    return pl.pallas_call(
        functools.partial(_peer_kernel, tT=tT, EB=EB),
        grid=(T // tT, PEER_N // EB),
        in_specs=[pl.BlockSpec((tT, D_MODEL), row),
                  pl.BlockSpec((PEER_HEADS * PEER_DQ, D_MODEL), full2, pipeline_mode=pl.Buffered(1)),
                  pl.BlockSpec((2 * PEER_HEADS, PEER_NKEYS, PEER_DQ // 2), lambda i, j: (0, 0, 0)),
                  pl.BlockSpec((EB, D_MODEL), lambda i, j: (j, 0)),
                  pl.BlockSpec((D_MODEL, EB), lambda i, j: (0, j)),
                  pl.BlockSpec((1, D_MODEL), full2),
                  pl.BlockSpec((1, D_MODEL), full2)],
        out_specs=pl.BlockSpec((tT, D_MODEL), row),
        out_shape=jax.ShapeDtypeStruct((T, D_MODEL), F32),
        scratch_shapes=[pltpu.VMEM((tT, D_MODEL), BF16),
                        pltpu.VMEM((PEER_HEADS * PEER_DQ, tT), BF16),
                        pltpu.VMEM((2 * PEER_HEADS, nl, PEER_NKEYS, LANES), F32),
                        pltpu.VMEM(head_shape, BF16),
                        pltpu.VMEM(head_shape, BF16),
                        pltpu.VMEM(head_shape, jnp.uint32),
                        pltpu.VMEM(head_shape, jnp.uint32),
                        pltpu.SMEM((PEER_HEADS * nl,), jnp.int32),
                        pltpu.VMEM((EB, tT), F32),
                        pltpu.VMEM((EB, tT), BF16),
                        pltpu.VMEM((D_MODEL, tT), F32)],
        compiler_params=pltpu.CompilerParams(
            dimension_semantics=("parallel", "arbitrary"), vmem_limit_bytes=VMEM_LIMIT),
        name="peer_ln",
    )(h, wq_t, sk, u, v_t, g.reshape(1, -1), b.reshape(1, -1))


def _tile(n, pref):
    return pref if n % pref == 0 else n


def _run_layer(l, h, Bn, S, prefix, params, tabs, lam, last_meta=False):
    (w_in, conv_w, subln_g, w_out, ln1_g, ln1_b, wq_t, sk, u, v_t, ln2_g, ln2_b) = params
    lam_init = 0.8 - 0.6 * math.exp(-0.3 * l)
    is_meta = prefix is None
    t = _tile(S, 256)
    ts = _tile(S, 512)
    halo = jnp.zeros((8, CONV_WIDTH), F32) if is_meta else prefix[2]
    outs = _proj_conv(h, w_in, conv_w, halo, Bn, S, ts, t, emit_u=is_meta)
    q, k, v, c = outs[:4]
    new_prefix = (k, v, outs[4][N_META - 8:N_META]) if is_meta else None
    if last_meta:
        return None, new_prefix
    kp, vp = (k, v) if is_meta else prefix[:2]
    a = _diff_attention(q, k, v, kp, vp, tabs[0], tabs[1], lam, subln_g, Bn, S, t,
                        has_prefix=not is_meta, lam_init=lam_init)
    T = Bn * S
    h1 = _outproj_ln(a, c, h, w_out, ln1_g, ln1_b, _tile(T, 512))
    h2 = _peer_ln(h1, wq_t, sk, u, v_t, ln2_g, ln2_b, _tile(T, 512), PEER_BLOCK)
    return h2, new_prefix


def kernel(x, meta_tokens, ln_in_g, ln_in_b, rel_bias, w_in, conv_w, lambda_q1, lambda_k1, lambda_q2,
           lambda_k2, subln_g, w_out, ln1_g, ln1_b, peer_w_q, peer_sub_keys, peer_u, peer_v, ln2_g, ln2_b):
    Bn, S, _ = x.shape
    depth = w_in.shape[0]
    w_in_b = w_in.astype(BF16)
    w_out_b = w_out.astype(BF16)
    wq_t = jnp.swapaxes(peer_w_q, 1, 2).astype(BF16)
    sk = peer_sub_keys.reshape(depth, 2 * PEER_HEADS, PEER_NKEYS, PEER_DQ // 2).astype(BF16)
    u_b = peer_u.astype(BF16)
    v_t = jnp.swapaxes(peer_v, 1, 2).astype(BF16)

    t_main = _tile(S, 256)
    tabs_main = _bias_tables(rel_bias, t_main)
    tabs_meta = _bias_tables(rel_bias, META_ROWS)

    meta = jnp.zeros((META_ROWS, D_MODEL), F32).at[:N_META].set(meta_tokens.astype(F32))
    hm = _layer_norm_rows(meta, ln_in_g, ln_in_b, META_ROWS)
    h = _layer_norm_rows(x.reshape(Bn * S, D_MODEL), ln_in_g, ln_in_b, _tile(Bn * S, 512))

    for l in range(depth):
        lam_init = 0.8 - 0.6 * math.exp(-0.3 * l)
        lam = _lambda(lambda_q1[l], lambda_k1[l], lambda_q2[l], lambda_k2[l], lam_init)
        params = (w_in_b[l], conv_w[l], subln_g[l], w_out_b[l], ln1_g[l], ln1_b[l], wq_t[l], sk[l],
                  u_b[l], v_t[l], ln2_g[l], ln2_b[l])
        hm, prefix = _run_layer(l, hm, 1, META_ROWS, None, params, tabs_meta, lam,
                                last_meta=(l == depth - 1))
        h, _ = _run_layer(l, h, Bn, S, prefix, params, tabs_main, lam)
    return h.reshape(Bn, S, D_MODEL)
```
